```python
import math
import jax
import jax.numpy as jnp
from jax import lax
import numpy as np

D_MODEL = 1024
BATCH = 4
SEQ = 8192
DEPTH = 4

HEAD_DIM = 64
DA_HEADS = 8
DA_VDIM = 2 * HEAD_DIM
SW_Q_HEADS = 16
SW_KV_HEADS = 4
SW_GROUP = SW_Q_HEADS // SW_KV_HEADS
WINDOW = 128
Q_BLOCK = 128
D_FF = 2816
CONV_W = 3
ROPE_THETA = 10000.0
NORM_EPS = 1e-6
SUBLN_EPS = 1e-5
N_BRANCH = 2
N_MOD = 6

DA_QK = DA_HEADS * 2 * HEAD_DIM
DA_V = DA_HEADS * DA_VDIM
SW_Q = SW_Q_HEADS * HEAD_DIM
SW_KV = SW_KV_HEADS * HEAD_DIM
IN_SIZES = (DA_QK, DA_QK, DA_V, SW_Q, SW_KV, SW_KV, N_BRANCH * D_MODEL)
IN_WIDTH = sum(IN_SIZES)
IN_SPLITS = tuple(int(s) for s in np.cumsum(IN_SIZES)[:-1])

kernel_name = 'hybrid_diffattn_swa_sinks_convffn'


def rmsnorm(x, g, eps=NORM_EPS):
    xf = x.astype(jnp.float32)
    y = xf * lax.rsqrt(jnp.mean(xf * xf, axis=-1, keepdims=True) + eps)
    return y.astype(x.dtype) * g


def rope_tables(positions):
    inv_freq = ROPE_THETA ** (-jnp.arange(0, HEAD_DIM, 2, dtype=jnp.float32) / HEAD_DIM)
    ang = positions.astype(jnp.float32)[..., None] * inv_freq
    return jnp.cos(ang), jnp.sin(ang)


def apply_rope(x, cos, sin):
    extra = x.ndim - 3
    shp = cos.shape[:2] + (1,) * extra + cos.shape[2:]
    cos = cos.reshape(shp).astype(x.dtype)
    sin = sin.reshape(shp).astype(x.dtype)
    x1, x2 = jnp.split(x, 2, axis=-1)
    return jnp.concatenate([x1 * cos - x2 * sin, x2 * cos + x1 * sin], axis=-1)


def diff_attention(q, k, v, lam, subln_g, lambda_init):
    B, S = q.shape[0], q.shape[1]
    nblk = S // Q_BLOCK
    scale = HEAD_DIM ** -0.5
    qb = q.reshape(B, nblk, Q_BLOCK, DA_HEADS, 2, HEAD_DIM).transpose(1, 0, 2, 3, 4, 5)
    kpos = jnp.arange(S)

    def one_block(args):
        qi, blk = args
        s = jnp.einsum('bqhcd,bkhcd->bhcqk', qi, k).astype(jnp.float32) * scale
        qpos = blk * Q_BLOCK + jnp.arange(Q_BLOCK)
        causal = kpos[None, :] <= qpos[:, None]
        s = jnp.where(causal, s, -jnp.inf)
        p = jax.nn.softmax(s, axis=-1)
        w = p[:, :, 0] - lam * p[:, :, 1]
        return jnp.einsum('bhqk,bkhe->bqhe', w.astype(v.dtype), v)

    o = lax.map(one_block, (qb, jnp.arange(nblk)))
    o = o.transpose(1, 0, 2, 3, 4).reshape(B, S, DA_HEADS, DA_VDIM)
    o = rmsnorm(o, subln_g, SUBLN_EPS) * (1.0 - lambda_init)
    return o.reshape(B, S, DA_V)


def sliding_window_attention(q, k, v, sinks):
    B, S = q.shape[0], q.shape[1]
    nblk = S // Q_BLOCK
    scale = HEAD_DIM ** -0.5
    qb = q.reshape(B, nblk, Q_BLOCK, SW_KV_HEADS, SW_GROUP, HEAD_DIM)
    pad = jnp.zeros((B, Q_BLOCK, SW_KV_HEADS, HEAD_DIM), k.dtype)
    kp = jnp.concatenate([pad, k], axis=1).reshape(B, nblk + 1, Q_BLOCK, SW_KV_HEADS, HEAD_DIM)
    vp = jnp.concatenate([pad, v], axis=1).reshape(B, nblk + 1, Q_BLOCK, SW_KV_HEADS, HEAD_DIM)
    kb = jnp.concatenate([kp[:, :-1], kp[:, 1:]], axis=2)
    vb = jnp.concatenate([vp[:, :-1], vp[:, 1:]], axis=2)
    s = jnp.einsum('bnqhgd,bnkhd->bnhgqk', qb, kb).astype(jnp.float32) * scale
    i = jnp.arange(Q_BLOCK)[:, None]
    j = jnp.arange(2 * Q_BLOCK)[None, :]
    band = (j > i) & (j <= i + WINDOW)
    kabs = jnp.arange(nblk)[:, None, None] * Q_BLOCK + j[None] - Q_BLOCK
    mask = band[None] & (kabs >= 0)
    s = jnp.where(mask[None, :, None, None], s, -jnp.inf)
    sink = jnp.broadcast_to(sinks.astype(jnp.float32).reshape(1, 1, SW_KV_HEADS, SW_GROUP, 1, 1),
                            s.shape[:-1] + (1,))
    p = jax.nn.softmax(jnp.concatenate([s, sink], axis=-1), axis=-1)[..., :-1]
    o = jnp.einsum('bnhgqk,bnkhd->bnqhgd', p.astype(v.dtype), vb)
    return o.reshape(B, S, SW_Q)


def conv_ffn(h, w_gate, w_up, conv_w, conv_b, w_down):
    S = h.shape[1]
    g = h @ w_gate
    u = h @ w_up
    gp = jnp.pad(g, ((0, 0), (CONV_W - 1, 0), (0, 0)))
    gc = conv_b + conv_w[0] * gp[:, 0:S]
    for t in range(1, CONV_W):
        gc = gc + conv_w[t] * gp[:, t:t + S]
    return (jax.nn.silu(gc) * u) @ w_down


def setup_inputs(seed: int = 0) -> dict:
    key = jax.random.key(seed)
    ks = jax.random.split(key, 20)
    f32 = jnp.float32
    nrm = lambda k, shape, s: (jax.random.normal(k, shape, f32) * s).astype(f32)
    x = nrm(ks[0], (BATCH, SEQ, D_MODEL), 1.0)
    c = nrm(ks[1], (BATCH, D_MODEL), 1.0)
    offset = jax.random.randint(ks[2], (BATCH, 1), 0, 4096, dtype=jnp.int32)
    positions = (offset + jnp.arange(SEQ, dtype=jnp.int32)[None, :]).astype(jnp.int32)
    return {
        'x': x,
        'c': c,
        'positions': positions,
        'w_ada': nrm(ks[3], (DEPTH, D_MODEL, N_MOD * D_MODEL), 0.5 * D_MODEL ** -0.5),
        'b_ada': nrm(ks[4], (DEPTH, N_MOD * D_MODEL), 0.02),
        'attn_norm': 1.0 + nrm(ks[5], (DEPTH, D_MODEL), 0.02),
        'w_in': nrm(ks[6], (DEPTH, D_MODEL, IN_WIDTH), D_MODEL ** -0.5),
        'lambda_qk': nrm(ks[7], (DEPTH, 4, HEAD_DIM), 0.1),
        'subln_norm': 1.0 + nrm(ks[8], (DEPTH, DA_VDIM), 0.02),
        'sinks': nrm(ks[9], (DEPTH, SW_Q_HEADS), 0.5),
        'w_branch_a': nrm(ks[10], (DEPTH, DA_V, D_MODEL), DA_V ** -0.5),
        'w_branch_b': nrm(ks[11], (DEPTH, SW_Q, D_MODEL), SW_Q ** -0.5),
        'w_out': nrm(ks[12], (DEPTH, D_MODEL, D_MODEL), D_MODEL ** -0.5),
        'ffn_norm': 1.0 + nrm(ks[13], (DEPTH, D_MODEL), 0.02),
        'w_gate': nrm(ks[14], (DEPTH, D_MODEL, D_FF), D_MODEL ** -0.5),
        'w_up': nrm(ks[15], (DEPTH, D_MODEL, D_FF), D_MODEL ** -0.5),
        'conv_w': nrm(ks[16], (DEPTH, CONV_W, D_FF), CONV_W ** -0.5),
        'conv_b': nrm(ks[17], (DEPTH, D_FF), 0.02),
        'w_down': nrm(ks[18], (DEPTH, D_FF, D_MODEL), D_FF ** -0.5),
        'final_norm': 1.0 + nrm(ks[19], (D_MODEL,), 0.02),
    }


def reference(x, c, positions, w_ada, b_ada, attn_norm, w_in, lambda_qk, subln_norm, sinks,
              w_branch_a, w_branch_b, w_out, ffn_norm, w_gate, w_up, conv_w, conv_b, w_down,
              final_norm):
    B, S = x.shape[0], x.shape[1]
    cos, sin = rope_tables(positions)
    c_act = jax.nn.silu(c)
    for l in range(DEPTH):
        mod = (c_act @ w_ada[l] + b_ada[l])[:, None, :]
        sh1, sc1, g1, sh2, sc2, g2 = jnp.split(mod, N_MOD, axis=-1)

        h = rmsnorm(x, attn_norm[l]) * (1.0 + sc1) + sh1
        proj = h @ w_in[l]
        qa, ka, va, qs, ksw, vsw, gates = jnp.split(proj, IN_SPLITS, axis=-1)

        lambda_init = 0.8 - 0.6 * math.exp(-0.3 * l)
        lq = lambda_qk[l].astype(jnp.float32)
        lam = jnp.exp(jnp.sum(lq[0] * lq[1])) - jnp.exp(jnp.sum(lq[2] * lq[3])) + lambda_init
        qa = apply_rope(qa.reshape(B, S, DA_HEADS, 2, HEAD_DIM), cos, sin)
        ka = apply_rope(ka.reshape(B, S, DA_HEADS, 2, HEAD_DIM), cos, sin)
        va = va.reshape(B, S, DA_HEADS, DA_VDIM)
        oa = diff_attention(qa, ka, va, lam, subln_norm[l], lambda_init)

        qs = apply_rope(qs.reshape(B, S, SW_Q_HEADS, HEAD_DIM), cos, sin)
        qs = qs.reshape(B, S, SW_KV_HEADS, SW_GROUP, HEAD_DIM)
        ksw = apply_rope(ksw.reshape(B, S, SW_KV_HEADS, HEAD_DIM), cos, sin)
        vsw = vsw.reshape(B, S, SW_KV_HEADS, HEAD_DIM)
        ob = sliding_window_attention(qs, ksw, vsw, sinks[l])

        ga, gb = jnp.split(jax.nn.sigmoid(gates), N_BRANCH, axis=-1)
        mixed = ga * (oa @ w_branch_a[l]) + gb * (ob @ w_branch_b[l])
        x = x + g1 * (mixed @ w_out[l])

        h = rmsnorm(x, ffn_norm[l]) * (1.0 + sc2) + sh2
        x = x + g2 * conv_ffn(h, w_gate[l], w_up[l], conv_w[l], conv_b[l], w_down[l])
    return rmsnorm(x, final_norm)
```

```python
import functools
import math

import jax
import jax.numpy as jnp
from jax import lax
from jax.experimental import pallas as pl
from jax.experimental.pallas import tpu as pltpu

F32 = jnp.float32
BF16 = jnp.bfloat16

HEAD_DIM = 64
DA_HEADS = 8
DA_VDIM = 2 * HEAD_DIM
SW_Q_HEADS = 16
SW_KV_HEADS = 4
SW_GROUP = SW_Q_HEADS // SW_KV_HEADS
WINDOW = 128
CONV_W = 3
ROPE_THETA = 10000.0
NORM_EPS = 1e-6
SUBLN_EPS = 1e-5
N_MOD = 6
QK_SCALE = HEAD_DIM ** -0.5

LANES = 128
BF16_ROWS = 16
VMEM_LIMIT = 56 * 1024 * 1024

NT_DIMS = (((1,), (1,)), ((), ()))


def _params(n_grid):
    return pltpu.CompilerParams(dimension_semantics=("arbitrary",) * n_grid,
                                vmem_limit_bytes=VMEM_LIMIT)


def _resident(shape, index_map):
    return pl.BlockSpec(shape, index_map, pipeline_mode=pl.Buffered(1))


def _rms(x, eps):
    return x * lax.rsqrt(jnp.mean(x * x, axis=-1, keepdims=True) + eps)


def _mod_kernel(c_ref, w_ref, b_ref, o_ref):
    c = c_ref[...]
    ca = c * jax.nn.sigmoid(c)
    o_ref[...] = jnp.dot(ca.astype(BF16), w_ref[...].astype(BF16),
                         preferred_element_type=F32) + b_ref[...]


def _adaln_mod(c, w_ada, b_ada):
    depth, d, width = w_ada.shape
    b = c.shape[0]
    rows = -(-b // 8) * 8
    c_pad = jnp.zeros((rows, d), F32).at[:b].set(c)
    tn = 1536
    out = pl.pallas_call(
        _mod_kernel,
        grid=(depth, width // tn),
        in_specs=[pl.BlockSpec((rows, d), lambda l, n: (0, 0)),
                  pl.BlockSpec((None, d, tn), lambda l, n: (l, 0, n)),
                  pl.BlockSpec((None, 1, tn), lambda l, n: (l, 0, n))],
        out_specs=pl.BlockSpec((None, rows, tn), lambda l, n: (l, 0, n)),
        out_shape=jax.ShapeDtypeStruct((depth, rows, width), F32),
        compiler_params=_params(2),
        name="adaln_mod",
    )(c_pad, w_ada, b_ada.reshape(depth, 1, width))
    return out[:, :b].reshape(depth, b, N_MOD, d)


def _rope_kernel(pos_ref, invf_ref, sign_ref, cos_ref, sin_ref):
    ang = pos_ref[...].astype(F32) * invf_ref[...]
    cos_ref[...] = jnp.cos(ang)
    sin_ref[...] = jnp.sin(ang) * sign_ref[...]


def _rope_tables(positions):
    b, s = positions.shape
    inv_freq = ROPE_THETA ** (-jnp.arange(0, HEAD_DIM, 2, dtype=F32) / HEAD_DIM)
    reps = LANES // (HEAD_DIM // 2)
    invf = jnp.tile(inv_freq, reps).reshape(1, LANES)
    half = HEAD_DIM // 2
    sign = jnp.tile(jnp.concatenate([-jnp.ones((half,), F32), jnp.ones((half,), F32)]),
                    LANES // HEAD_DIM).reshape(1, LANES)
    tm = min(s, 1024)
    return pl.pallas_call(
        _rope_kernel,
        grid=(b, s // tm),
        in_specs=[pl.BlockSpec((None, tm, 1), lambda bi, i: (bi, i, 0)),
                  pl.BlockSpec((1, LANES), lambda bi, i: (0, 0)),
                  pl.BlockSpec((1, LANES), lambda bi, i: (0, 0))],
        out_specs=[pl.BlockSpec((None, tm, LANES), lambda bi, i: (bi, i, 0))] * 2,
        out_shape=[jax.ShapeDtypeStruct((b, s, LANES), F32)] * 2,
        compiler_params=_params(2),
        name="rope_tables",
    )(positions.reshape(b, s, 1), invf, sign)


def _rope_block(a, cos, sin_signed, first_half):
    swapped = jnp.where(first_half, pltpu.roll(a, LANES - HEAD_DIM // 2, axis=1),
                        pltpu.roll(a, HEAD_DIM // 2, axis=1))
    return a * cos + swapped * sin_signed


def _inproj_kernel(x_ref, mod_ref, g_ref, cos_ref, sin_ref, w_ref,
                   qa_ref, ka_ref, va_ref, qs_ref, ks_ref, vs_ref, gate_ref, h_ref,
                   *, sections, chunk):
    x = x_ref[...]
    mod = mod_ref[...]
    shift, scale = mod[0:1], mod[1:2]
    h = _rms(x, NORM_EPS) * g_ref[...] * (1.0 + scale) + shift
    h_ref[...] = h.astype(BF16)

    cos = cos_ref[...]
    sin_signed = sin_ref[...]
    lane = lax.broadcasted_iota(jnp.int32, cos.shape, 1)
    first_half = (lane % HEAD_DIM) < (HEAD_DIM // 2)

    outs = (qa_ref, ka_ref, va_ref, qs_ref, ks_ref, vs_ref, gate_ref)
    col = 0
    for out_ref, (width, kind, mult) in zip(outs, sections):
        for c0 in range(0, width, chunk):
            cw = min(chunk, width - c0)
            acc = jnp.dot(h_ref[...], w_ref[:, col + c0:col + c0 + cw],
                          preferred_element_type=F32)
            if kind == "rope":
                for j in range(cw // LANES):
                    blk = _rope_block(acc[:, j * LANES:(j + 1) * LANES], cos, sin_signed,
                                      first_half)
                    if mult != 1.0:
                        blk = blk * mult
                    out_ref[:, c0 + j * LANES:c0 + (j + 1) * LANES] = blk.astype(BF16)
            elif kind == "sigmoid":
                out_ref[:, c0:c0 + cw] = jax.nn.sigmoid(acc).astype(BF16)
            else:
                out_ref[:, c0:c0 + cw] = acc.astype(BF16)
        col += width


def _in_projection(x, mod_l, g, cos, sin_signed, w_in_bf, layer, tm):
    b, s, d = x.shape
    da_qk = DA_HEADS * 2 * HEAD_DIM
    da_v = DA_HEADS * DA_VDIM
    sw_q = SW_Q_HEADS * HEAD_DIM
    sw_kv = SW_KV_HEADS * HEAD_DIM
    sections = ((da_qk, "rope", QK_SCALE), (da_qk, "rope", 1.0), (da_v, "plain", 1.0),
                (sw_q, "rope", QK_SCALE), (sw_kv, "rope", 1.0), (sw_kv, "plain", 1.0),
                (2 * d, "sigmoid", 1.0))
    widths = [w for w, _, _ in sections]
    in_width = sum(widths)
    assert w_in_bf.shape[1:] == (d, in_width)
    row = lambda bi, i: (bi, i, 0)
    return pl.pallas_call(
        functools.partial(_inproj_kernel, sections=sections, chunk=512),
        grid=(b, s // tm),
        in_specs=[pl.BlockSpec((None, tm, d), row),
                  pl.BlockSpec((None, N_MOD, d), lambda bi, i: (bi, 0, 0)),
                  pl.BlockSpec((None, 1, d), lambda bi, i: (layer, 0, 0)),
                  pl.BlockSpec((None, tm, LANES), row),
                  pl.BlockSpec((None, tm, LANES), row),
                  _resident((None, d, in_width), lambda bi, i: (layer, 0, 0))],
        out_specs=[pl.BlockSpec((None, tm, w), row) for w in widths],
        out_shape=[jax.ShapeDtypeStruct((b, s, w), BF16) for w in widths],
        scratch_shapes=[pltpu.VMEM((tm, d), BF16)],
        compiler_params=_params(2),
        name="in_projection",
    )(x, mod_l, g, cos, sin_signed, w_in_bf)


def _softmax_step(s, v, m_ref, l_ref, acc_ref):
    m_old = m_ref[...]
    m_new = jnp.maximum(m_old, jnp.max(s, axis=1, keepdims=True))
    alpha = jnp.exp(m_old - m_new)
    p = jnp.exp(s - m_new)
    l_ref[...] = alpha * l_ref[...] + jnp.sum(p, axis=1, keepdims=True)
    acc_ref[...] = alpha * acc_ref[...] + jnp.dot(p.astype(BF16), v, preferred_element_type=F32)
    m_ref[...] = m_new


def _diffattn_kernel(q_ref, k_ref, v_ref, lam_ref, g_ref, o_ref,
                     m0_ref, l0_ref, a0_ref, m1_ref, l1_ref, a1_ref, *, lambda_init, tq):
    i = pl.program_id(2)
    q = q_ref[...]
    lane = lax.broadcasted_iota(jnp.int32, q.shape, 1)
    zero = jnp.zeros_like(q)
    q0 = jnp.where(lane < HEAD_DIM, q, zero)
    q1 = jnp.where(lane >= HEAD_DIM, q, zero)

    for m_ref, l_ref, a_ref in ((m0_ref, l0_ref, a0_ref), (m1_ref, l1_ref, a1_ref)):
        m_ref[...] = jnp.full(m_ref.shape, -jnp.inf, F32)
        l_ref[...] = jnp.zeros(l_ref.shape, F32)
        a_ref[...] = jnp.zeros(a_ref.shape, F32)

    def block(j, masked):
        start = pl.multiple_of(j * tq, tq)
        k = k_ref[pl.ds(start, tq), :]
        v = v_ref[pl.ds(start, tq), :]
        s0 = lax.dot_general(q0, k, NT_DIMS, preferred_element_type=F32)
        s1 = lax.dot_general(q1, k, NT_DIMS, preferred_element_type=F32)
        if masked:
            row = lax.broadcasted_iota(jnp.int32, s0.shape, 0)
            colm = lax.broadcasted_iota(jnp.int32, s0.shape, 1)
            keep = colm <= row
            s0 = jnp.where(keep, s0, -jnp.inf)
            s1 = jnp.where(keep, s1, -jnp.inf)
        _softmax_step(s0, v, m0_ref, l0_ref, a0_ref)
        _softmax_step(s1, v, m1_ref, l1_ref, a1_ref)

    def body(j, carry):
        block(j, masked=False)
        return carry

    lax.fori_loop(0, i, body, 0)
    block(i, masked=True)

    lq = lam_ref[...]
    lam = (jnp.exp(jnp.sum(lq[0:1] * lq[1:2], axis=1, keepdims=True))
           - jnp.exp(jnp.sum(lq[2:3] * lq[3:4], axis=1, keepdims=True)) + lambda_init)
    o = a0_ref[...] / l0_ref[...] - lam * (a1_ref[...] / l1_ref[...])
    o = _rms(o, SUBLN_EPS) * g_ref[...] * (1.0 - lambda_init)
    o_ref[...] = o.astype(BF16)


def _diff_attention(qa, ka, va, lambda_qk, subln, layer, lambda_init, tq):
    b, s, width = qa.shape
    heads = width // DA_VDIM
    kv_spec = pl.BlockSpec((None, s, DA_VDIM), lambda bi, h, i: (bi, 0, h))
    stat = pltpu.VMEM((tq, 1), F32)
    acc = pltpu.VMEM((tq, DA_VDIM), F32)
    return pl.pallas_call(
        functools.partial(_diffattn_kernel, lambda_init=lambda_init, tq=tq),
        grid=(b, heads, s // tq),
        in_specs=[pl.BlockSpec((None, tq, DA_VDIM), lambda bi, h, i: (bi, i, h)),
                  kv_spec, kv_spec,
                  pl.BlockSpec((None, 4, HEAD_DIM), lambda bi, h, i: (layer, 0, 0)),
                  pl.BlockSpec((None, 1, DA_VDIM), lambda bi, h, i: (layer, 0, 0))],
        out_specs=pl.BlockSpec((None, tq, DA_VDIM), lambda bi, h, i: (bi, i, h)),
        out_shape=jax.ShapeDtypeStruct((b, s, width), BF16),
        scratch_shapes=[stat, stat, acc, stat, stat, acc],
        compiler_params=_params(3),
        name="diff_attention",
    )(qa, ka, va, lambda_qk, subln)


def _swa_kernel(q_ref, kc_ref, vc_ref, kp_ref, vp_ref, sink_ref, o_ref, *, tq):
    i = pl.program_id(1)
    kcat = jnp.concatenate([kp_ref[...], kc_ref[...]], axis=0)
    vcat = jnp.concatenate([vp_ref[...], vc_ref[...]], axis=0)
    qi = lax.broadcasted_iota(jnp.int32, (WINDOW, 2 * WINDOW), 0)
    kj = lax.broadcasted_iota(jnp.int32, (WINDOW, 2 * WINDOW), 1)
    band = (kj > qi) & (kj <= qi + WINDOW)
    band_first = band & (kj >= jnp.where(i > 0, 0, WINDOW))
    sinks = sink_ref[...]
    for r in range(tq // WINDOW):
        mask = band_first if r == 0 else band
        rows = slice(r * WINDOW, (r + 1) * WINDOW)
        for kvh in range(SW_KV_HEADS):
            cols = slice(kvh * HEAD_DIM, (kvh + 1) * HEAD_DIM)
            k = kcat[r * WINDOW:(r + 2) * WINDOW, cols]
            v = vcat[r * WINDOW:(r + 2) * WINDOW, cols]
            outs = []
            for g in range(SW_GROUP):
                head = kvh * SW_GROUP + g
                q = q_ref[rows, head * HEAD_DIM:(head + 1) * HEAD_DIM]
                s = lax.dot_general(q, k, NT_DIMS, preferred_element_type=F32)
                s = jnp.where(mask, s, -jnp.inf)
                sink = sinks[:, head:head + 1]
                m = jnp.maximum(jnp.max(s, axis=1, keepdims=True), sink)
                p = jnp.exp(s - m)
                denom = jnp.sum(p, axis=1, keepdims=True) + jnp.exp(sink - m)
                o = jnp.dot(p.astype(BF16), v, preferred_element_type=F32) / denom
                outs.append(o.astype(BF16))
            o_ref[rows, kvh * SW_GROUP * HEAD_DIM:(kvh + 1) * SW_GROUP * HEAD_DIM] = (
                jnp.concatenate(outs, axis=1))


def _swa_attention(qs, ksw, vsw, sinks, layer, tq):
    b, s, width = qs.shape
    kv_width = ksw.shape[-1]
    per = tq // WINDOW
    row = lambda bi, i: (bi, i, 0)
    prev = lambda bi, i: (bi, jnp.maximum(i * per - 1, 0), 0)
    return pl.pallas_call(
        functools.partial(_swa_kernel, tq=tq),
        grid=(b, s // tq),
        in_specs=[pl.BlockSpec((None, tq, width), row),
                  pl.BlockSpec((None, tq, kv_width), row),
                  pl.BlockSpec((None, tq, kv_width), row),
                  pl.BlockSpec((None, WINDOW, kv_width), prev),
                  pl.BlockSpec((None, WINDOW, kv_width), prev),
                  pl.BlockSpec((None, 1, SW_Q_HEADS), lambda bi, i: (layer, 0, 0))],
        out_specs=pl.BlockSpec((None, tq, width), row),
        out_shape=jax.ShapeDtypeStruct((b, s, width), BF16),
        compiler_params=_params(2),
        name="swa_attention",
    )(qs, ksw, vsw, ksw, vsw, sinks)


def _merge_kernel(x_ref, oa_ref, ob_ref, gate_ref, mod_ref, wa_ref, wb_ref, wo_ref, o_ref):
    d = x_ref.shape[-1]
    a = jnp.dot(oa_ref[...], wa_ref[...], preferred_element_type=F32)
    bb = jnp.dot(ob_ref[...], wb_ref[...], preferred_element_type=F32)
    mixed = gate_ref[:, :d].astype(F32) * a + gate_ref[:, d:].astype(F32) * bb
    y = jnp.dot(mixed.astype(BF16), wo_ref[...], preferred_element_type=F32)
    o_ref[...] = x_ref[...] + mod_ref[2:3, :] * y


def _merge_out(x, oa, ob, gates, mod_l, wa_bf, wb_bf, wo_bf, layer, tm):
    b, s, d = x.shape
    row = lambda bi, i: (bi, i, 0)
    wspec = lambda w: _resident((None,) + w.shape[1:], lambda bi, i: (layer, 0, 0))
    return pl.pallas_call(
        _merge_kernel,
        grid=(b, s // tm),
        in_specs=[pl.BlockSpec((None, tm, d), row),
                  pl.BlockSpec((None, tm, oa.shape[-1]), row),
                  pl.BlockSpec((None, tm, ob.shape[-1]), row),
                  pl.BlockSpec((None, tm, 2 * d), row),
                  pl.BlockSpec((None, N_MOD, d), lambda bi, i: (bi, 0, 0)),
                  wspec(wa_bf), wspec(wb_bf), wspec(wo_bf)],
        out_specs=pl.BlockSpec((None, tm, d), row),
        out_shape=jax.ShapeDtypeStruct((b, s, d), F32),
        compiler_params=_params(2),
        name="merge_out",
    )(x, oa, ob, gates, mod_l, wa_bf, wb_bf, wo_bf)


def _ffn_kernel(x_ref, halo_ref, mod_ref, g_ref, wg_ref, wu_ref, cw_ref, cb_ref, wd_ref, o_ref,
                h_ref, acc_ref, *, tm, n_chunks):
    i = pl.program_id(1)
    mod = mod_ref[...]
    shift, scale, gate = mod[3:4], mod[4:5], mod[5:6]
    gain = g_ref[...]

    def normed(rows):
        return _rms(rows, NORM_EPS) * gain * (1.0 + scale) + shift

    x = x_ref[...]
    halo = jnp.where(i > 0, normed(halo_ref[...]), 0.0)
    h_ref[0:BF16_ROWS, :] = halo.astype(BF16)
    h_ref[BF16_ROWS:, :] = normed(x).astype(BF16)
    acc_ref[...] = jnp.zeros(acc_ref.shape, F32)

    def chunk(c, carry):
        h = h_ref[...]
        g = jnp.dot(h, wg_ref[c], preferred_element_type=F32)
        u = jnp.dot(h[BF16_ROWS:], wu_ref[c], preferred_element_type=F32)
        cw = cw_ref[c]
        gc = cb_ref[c] + cw[CONV_W - 1:CONV_W] * g[BF16_ROWS:]
        for t in range(CONV_W - 1):
            back = CONV_W - 1 - t
            gc = gc + cw[t:t + 1] * pltpu.roll(g, back, axis=0)[BF16_ROWS:]
        act = (gc * jax.nn.sigmoid(gc)) * u
        acc_ref[...] += jnp.dot(act.astype(BF16), wd_ref[c], preferred_element_type=F32)
        return carry

    lax.fori_loop(0, n_chunks, chunk, 0)
    o_ref[...] = x + gate * acc_ref[...]


def _conv_ffn(x, mod_l, g, wg_c, wu_c, cw_c, cb_c, wd_c, layer, tm):
    b, s, d = x.shape
    n_chunks, _, cw = wg_c.shape[1:]
    per = tm // BF16_ROWS
    row = lambda bi, i: (bi, i, 0)
    wspec = lambda w: _resident((None,) + w.shape[1:], lambda bi, i: (layer, 0, 0, 0))
    return pl.pallas_call(
        functools.partial(_ffn_kernel, tm=tm, n_chunks=n_chunks),
        grid=(b, s // tm),
        in_specs=[pl.BlockSpec((None, tm, d), row),
                  pl.BlockSpec((None, BF16_ROWS, d),
                               lambda bi, i: (bi, jnp.maximum(i * per - 1, 0), 0)),
                  pl.BlockSpec((None, N_MOD, d), lambda bi, i: (bi, 0, 0)),
                  pl.BlockSpec((None, 1, d), lambda bi, i: (layer, 0, 0)),
                  wspec(wg_c), wspec(wu_c), wspec(cw_c), wspec(cb_c), wspec(wd_c)],
        out_specs=pl.BlockSpec((None, tm, d), row),
        out_shape=jax.ShapeDtypeStruct((b, s, d), F32),
        scratch_shapes=[pltpu.VMEM((BF16_ROWS + tm, d), BF16), pltpu.VMEM((tm, d), F32)],
        compiler_params=_params(2),
        name="conv_ffn",
    )(x, x, mod_l, g, wg_c, wu_c, cw_c, cb_c, wd_c)


def _final_norm_kernel(x_ref, g_ref, o_ref):
    o_ref[...] = _rms(x_ref[...], NORM_EPS) * g_ref[...]


def _final_norm(x, g, tm):
    b, s, d = x.shape
    row = lambda bi, i: (bi, i, 0)
    return pl.pallas_call(
        _final_norm_kernel,
        grid=(b, s // tm),
        in_specs=[pl.BlockSpec((None, tm, d), row), pl.BlockSpec((1, d), lambda bi, i: (0, 0))],
        out_specs=pl.BlockSpec((None, tm, d), row),
        out_shape=jax.ShapeDtypeStruct((b, s, d), F32),
        compiler_params=_params(2),
        name="final_norm",
    )(x, g.reshape(1, d))


def _ffn_chunk_width(d_ff):
    for cw in (512, 384, 256, 128):
        if d_ff % cw == 0:
            return cw
    raise ValueError(f"d_ff={d_ff} is not a multiple of {LANES}")


def kernel(x, c, positions, w_ada, b_ada, attn_norm, w_in, lambda_qk, subln_norm, sinks,
           w_branch_a, w_branch_b, w_out, ffn_norm, w_gate, w_up, conv_w, conv_b, w_down,
           final_norm):
    b, s, d = x.shape
    depth = w_in.shape[0]
    d_ff = w_gate.shape[-1]
    tm = min(s, 512)
    tq_da = min(s, 512)
    tq_sw = min(s, 256)

    mod = _adaln_mod(c, w_ada, b_ada)
    cos, sin_signed = _rope_tables(positions)

    cw = _ffn_chunk_width(d_ff)
    nc = d_ff // cw
    w_in_bf = w_in.astype(BF16)
    wa_bf, wb_bf, wo_bf = (w.astype(BF16) for w in (w_branch_a, w_branch_b, w_out))
    wg_c = w_gate.astype(BF16).reshape(depth, d, nc, cw).transpose(0, 2, 1, 3)
    wu_c = w_up.astype(BF16).reshape(depth, d, nc, cw).transpose(0, 2, 1, 3)
    wd_c = w_down.astype(BF16).reshape(depth, nc, cw, d)
    cw_c = conv_w.reshape(depth, CONV_W, nc, cw).transpose(0, 2, 1, 3)
    cb_c = conv_b.reshape(depth, nc, 1, cw)
    attn_g = attn_norm.reshape(depth, 1, d)
    ffn_g = ffn_norm.reshape(depth, 1, d)
    subln = subln_norm.reshape(depth, 1, DA_VDIM)
    sinks3 = sinks.reshape(depth, 1, SW_Q_HEADS)

    for l in range(depth):
        lambda_init = 0.8 - 0.6 * math.exp(-0.3 * l)
        mod_l = mod[l]
        qa, ka, va, qs, ksw, vsw, gates = _in_projection(x, mod_l, attn_g, cos, sin_signed,
                                                        w_in_bf, l, tm)
        oa = _diff_attention(qa, ka, va, lambda_qk, subln, l, lambda_init, tq_da)
        ob = _swa_attention(qs, ksw, vsw, sinks3, l, tq_sw)
        x = _merge_out(x, oa, ob, gates, mod_l, wa_bf, wb_bf, wo_bf, l, tm)
        x = _conv_ffn(x, mod_l, ffn_g, wg_c, wu_c, cw_c, cb_c, wd_c, l, tm)
    return _final_norm(x, final_norm, tm)
```

```python
import functools
import math

import jax
import jax.numpy as jnp
from jax import lax
from jax.experimental import pallas as pl
from jax.experimental.pallas import tpu as pltpu

F32 = jnp.float32
BF16 = jnp.bfloat16

HEAD_DIM = 64
DA_HEADS = 8
DA_VDIM = 2 * HEAD_DIM
SW_Q_HEADS = 16
SW_KV_HEADS = 4
SW_GROUP = SW_Q_HEADS // SW_KV_HEADS
WINDOW = 128
CONV_W = 3
ROPE_THETA = 10000.0
NORM_EPS = 1e-6
SUBLN_EPS = 1e-5
N_MOD = 6
QK_SCALE = HEAD_DIM ** -0.5
LOG2_E = math.log2(math.e)

LANES = 128
BF16_ROWS = 16
VMEM_LIMIT = 56 * 1024 * 1024

NT_DIMS = (((1,), (1,)), ((), ()))


def _params(n_grid):
    return pltpu.CompilerParams(dimension_semantics=("arbitrary",) * n_grid,
                                vmem_limit_bytes=VMEM_LIMIT)


def _resident(shape, index_map):
    return pl.BlockSpec(shape, index_map, pipeline_mode=pl.Buffered(1))


def _rms(x, eps):
    return x * lax.rsqrt(jnp.mean(x * x, axis=-1, keepdims=True) + eps)


def _mod_kernel(c_ref, w_ref, b_ref, o_ref):
    c = c_ref[...]
    ca = c * jax.nn.sigmoid(c)
    o_ref[...] = jnp.dot(ca.astype(BF16), w_ref[...].astype(BF16),
                         preferred_element_type=F32) + b_ref[...]


def _adaln_mod(c, w_ada, b_ada):
    depth, d, width = w_ada.shape
    b = c.shape[0]
    rows = -(-b // 8) * 8
    c_pad = jnp.zeros((rows, d), F32).at[:b].set(c)
    tn = 1536
    out = pl.pallas_call(
        _mod_kernel,
        grid=(depth, width // tn),
        in_specs=[pl.BlockSpec((rows, d), lambda l, n: (0, 0)),
                  pl.BlockSpec((None, d, tn), lambda l, n: (l, 0, n)),
                  pl.BlockSpec((None, 1, tn), lambda l, n: (l, 0, n))],
        out_specs=pl.BlockSpec((None, rows, tn), lambda l, n: (l, 0, n)),
        out_shape=jax.ShapeDtypeStruct((depth, rows, width), F32),
        compiler_params=_params(2),
        name="adaln_mod",
    )(c_pad, w_ada, b_ada.reshape(depth, 1, width))
    return out[:, :b].reshape(depth, b, N_MOD, d)


def _rope_kernel(pos_ref, invf_ref, sign_ref, cos_ref, sin_ref):
    ang = pos_ref[...].astype(F32) * invf_ref[...]
    cos_ref[...] = jnp.cos(ang)
    sin_ref[...] = jnp.sin(ang) * sign_ref[...]


def _rope_tables(positions):
    b, s = positions.shape
    inv_freq = ROPE_THETA ** (-jnp.arange(0, HEAD_DIM, 2, dtype=F32) / HEAD_DIM)
    reps = LANES // (HEAD_DIM // 2)
    invf = jnp.tile(inv_freq, reps).reshape(1, LANES)
    half = HEAD_DIM // 2
    sign = jnp.tile(jnp.concatenate([-jnp.ones((half,), F32), jnp.ones((half,), F32)]),
                    LANES // HEAD_DIM).reshape(1, LANES)
    tm = min(s, 1024)
    return pl.pallas_call(
        _rope_kernel,
        grid=(b, s // tm),
        in_specs=[pl.BlockSpec((None, tm, 1), lambda bi, i: (bi, i, 0)),
                  pl.BlockSpec((1, LANES), lambda bi, i: (0, 0)),
                  pl.BlockSpec((1, LANES), lambda bi, i: (0, 0))],
        out_specs=[pl.BlockSpec((None, tm, LANES), lambda bi, i: (bi, i, 0))] * 2,
        out_shape=[jax.ShapeDtypeStruct((b, s, LANES), F32)] * 2,
        compiler_params=_params(2),
        name="rope_tables",
    )(positions.reshape(b, s, 1), invf, sign)


def _rope_block(a, cos, sin_signed, first_half):
    swapped = jnp.where(first_half, pltpu.roll(a, LANES - HEAD_DIM // 2, axis=1),
                        pltpu.roll(a, HEAD_DIM // 2, axis=1))
    return a * cos + swapped * sin_signed


def _inproj_kernel(x_ref, mod_ref, g_ref, cos_ref, sin_ref, w_ref,
                   qa_ref, ka_ref, va_ref, qs_ref, ks_ref, vs_ref, gate_ref, h_ref,
                   *, sections, chunk):
    x = x_ref[...]
    mod = mod_ref[...]
    shift, scale = mod[0:1], mod[1:2]
    h = _rms(x, NORM_EPS) * g_ref[...] * (1.0 + scale) + shift
    h_ref[...] = h.astype(BF16)

    cos = cos_ref[...]
    sin_signed = sin_ref[...]
    lane = lax.broadcasted_iota(jnp.int32, cos.shape, 1)
    first_half = (lane % HEAD_DIM) < (HEAD_DIM // 2)

    outs = (qa_ref, ka_ref, va_ref, qs_ref, ks_ref, vs_ref, gate_ref)
    col = 0
    for out_ref, (width, kind, mult) in zip(outs, sections):
        for c0 in range(0, width, chunk):
            cw = min(chunk, width - c0)
            acc = jnp.dot(h_ref[...], w_ref[:, col + c0:col + c0 + cw],
                          preferred_element_type=F32)
            if kind == "rope":
                for j in range(cw // LANES):
                    blk = _rope_block(acc[:, j * LANES:(j + 1) * LANES], cos, sin_signed,
                                      first_half)
                    if mult != 1.0:
                        blk = blk * mult
                    out_ref[:, c0 + j * LANES:c0 + (j + 1) * LANES] = blk.astype(BF16)
            elif kind == "sigmoid":
                out_ref[:, c0:c0 + cw] = jax.nn.sigmoid(acc).astype(BF16)
            else:
                out_ref[:, c0:c0 + cw] = acc.astype(BF16)
        col += width


def _in_projection(x, mod_l, g, cos, sin_signed, w_in_bf, layer, tm):
    b, s, d = x.shape
    da_qk = DA_HEADS * 2 * HEAD_DIM
    da_v = DA_HEADS * DA_VDIM
    sw_q = SW_Q_HEADS * HEAD_DIM
    sw_kv = SW_KV_HEADS * HEAD_DIM
    sections = ((da_qk, "rope", QK_SCALE * LOG2_E), (da_qk, "rope", 1.0), (da_v, "plain", 1.0),
                (sw_q, "rope", QK_SCALE), (sw_kv, "rope", 1.0), (sw_kv, "plain", 1.0),
                (2 * d, "sigmoid", 1.0))
    widths = [w for w, _, _ in sections]
    in_width = sum(widths)
    assert w_in_bf.shape[1:] == (d, in_width)
    row = lambda bi, i: (bi, i, 0)
    return pl.pallas_call(
        functools.partial(_inproj_kernel, sections=sections, chunk=512),
        grid=(b, s // tm),
        in_specs=[pl.BlockSpec((None, tm, d), row),
                  pl.BlockSpec((None, N_MOD, d), lambda bi, i: (bi, 0, 0)),
                  pl.BlockSpec((None, 1, d), lambda bi, i: (layer, 0, 0)),
                  pl.BlockSpec((None, tm, LANES), row),
                  pl.BlockSpec((None, tm, LANES), row),
                  _resident((None, d, in_width), lambda bi, i: (layer, 0, 0))],
        out_specs=[pl.BlockSpec((None, tm, w), row) for w in widths],
        out_shape=[jax.ShapeDtypeStruct((b, s, w), BF16) for w in widths],
        scratch_shapes=[pltpu.VMEM((tm, d), BF16)],
        compiler_params=_params(2),
        name="in_projection",
    )(x, mod_l, g, cos, sin_signed, w_in_bf)


def _softmax_step(s, v_ones, m_ref, acc_ref):
    reps = s.shape[1] // LANES
    m_old = m_ref[...]
    m_new = jnp.maximum(m_old, jnp.max(s, axis=1, keepdims=True))
    alpha = jnp.exp2(m_old - m_new)
    p = jnp.exp2(s - jnp.concatenate([m_new] * reps, axis=1))
    pv = jnp.dot(p.astype(BF16), v_ones, preferred_element_type=F32)
    acc_ref[...] = jnp.concatenate([alpha, alpha], axis=1) * acc_ref[...] + pv
    m_ref[...] = m_new


def _diffattn_kernel(q_ref, k_ref, v_ref, lam_ref, g_ref, o_ref,
                     m0_ref, a0_ref, m1_ref, a1_ref, *, lambda_init, tq, tk):
    i = pl.program_id(2)
    q = q_ref[...]
    lane = lax.broadcasted_iota(jnp.int32, q.shape, 1)
    zero = jnp.zeros_like(q)
    q0 = jnp.where(lane < HEAD_DIM, q, zero)
    q1 = jnp.where(lane >= HEAD_DIM, q, zero)

    for m_ref, a_ref in ((m0_ref, a0_ref), (m1_ref, a1_ref)):
        m_ref[...] = jnp.full(m_ref.shape, -jnp.inf, F32)
        a_ref[...] = jnp.zeros(a_ref.shape, F32)

    def block(start, size, masked):
        k = k_ref[pl.ds(start, size), :]
        v = v_ref[pl.ds(start, size), :]
        v_ones = jnp.concatenate([v, jnp.ones_like(v)], axis=1)
        s0 = lax.dot_general(q0, k, NT_DIMS, preferred_element_type=F32)
        s1 = lax.dot_general(q1, k, NT_DIMS, preferred_element_type=F32)
        if masked:
            row = lax.broadcasted_iota(jnp.int32, s0.shape, 0)
            colm = lax.broadcasted_iota(jnp.int32, s0.shape, 1)
            keep = colm <= row
            s0 = jnp.where(keep, s0, -jnp.inf)
            s1 = jnp.where(keep, s1, -jnp.inf)
        _softmax_step(s0, v_ones, m0_ref, a0_ref)
        _softmax_step(s1, v_ones, m1_ref, a1_ref)

    def body(j, carry):
        block(pl.multiple_of(j * tk, tk), tk, masked=False)
        return carry

    lax.fori_loop(0, i * (tq // tk), body, 0)
    block(pl.multiple_of(i * tq, tq), tq, masked=True)

    lq = lam_ref[...]
    lam = (jnp.exp(jnp.sum(lq[0:1] * lq[1:2], axis=1, keepdims=True))
           - jnp.exp(jnp.sum(lq[2:3] * lq[3:4], axis=1, keepdims=True)) + lambda_init)
    a0 = a0_ref[...]
    a1 = a1_ref[...]
    o = a0[:, :DA_VDIM] / a0[:, DA_VDIM:] - lam * (a1[:, :DA_VDIM] / a1[:, DA_VDIM:])
    o = _rms(o, SUBLN_EPS) * g_ref[...] * (1.0 - lambda_init)
    o_ref[...] = o.astype(BF16)


def _diff_attention(qa, ka, va, lambda_qk, subln, layer, lambda_init, tq, tk):
    b, s, width = qa.shape
    heads = width // DA_VDIM
    assert tq % tk == 0 and DA_VDIM == LANES
    kv_spec = pl.BlockSpec((None, s, DA_VDIM), lambda bi, h, i: (bi, 0, h))
    stat = pltpu.VMEM((tq, LANES), F32)
    acc = pltpu.VMEM((tq, 2 * DA_VDIM), F32)
    return pl.pallas_call(
        functools.partial(_diffattn_kernel, lambda_init=lambda_init, tq=tq, tk=tk),
        grid=(b, heads, s // tq),
        in_specs=[pl.BlockSpec((None, tq, DA_VDIM), lambda bi, h, i: (bi, i, h)),
                  kv_spec, kv_spec,
                  pl.BlockSpec((None, 4, HEAD_DIM), lambda bi, h, i: (layer, 0, 0)),
                  pl.BlockSpec((None, 1, DA_VDIM), lambda bi, h, i: (layer, 0, 0))],
        out_specs=pl.BlockSpec((None, tq, DA_VDIM), lambda bi, h, i: (bi, i, h)),
        out_shape=jax.ShapeDtypeStruct((b, s, width), BF16),
        scratch_shapes=[stat, acc, stat, acc],
        compiler_params=_params(3),
        name="diff_attention",
    )(qa, ka, va, lambda_qk, subln)


def _swa_kernel(q_ref, kc_ref, vc_ref, kp_ref, vp_ref, sink_ref, o_ref, *, tq):
    i = pl.program_id(1)
    kcat = jnp.concatenate([kp_ref[...], kc_ref[...]], axis=0)
    vcat = jnp.concatenate([vp_ref[...], vc_ref[...]], axis=0)
    stacked = (SW_GROUP * WINDOW, 2 * WINDOW)
    qi = lax.broadcasted_iota(jnp.int32, stacked, 0) % WINDOW
    kj = lax.broadcasted_iota(jnp.int32, stacked, 1)
    band = (kj > qi) & (kj <= qi + WINDOW)
    band_first = band & (kj >= jnp.where(i > 0, 0, WINDOW))
    sinks = sink_ref[...]
    ones = jnp.ones((2 * WINDOW, LANES), BF16)
    pad = jnp.zeros((2 * WINDOW, LANES - HEAD_DIM), BF16)
    group_w = SW_GROUP * HEAD_DIM
    for kvh in range(SW_KV_HEADS):
        cols = slice(kvh * HEAD_DIM, (kvh + 1) * HEAD_DIM)
        sink_col = jnp.concatenate(
            [jnp.broadcast_to(sinks[:, kvh * SW_GROUP + g:kvh * SW_GROUP + g + 1], (WINDOW, LANES))
             for g in range(SW_GROUP)], axis=0)
        for r in range(tq // WINDOW):
            mask = band_first if r == 0 else band
            rows = slice(r * WINDOW, (r + 1) * WINDOW)
            k = kcat[r * WINDOW:(r + 2) * WINDOW, cols]
            v = vcat[r * WINDOW:(r + 2) * WINDOW, cols]
            v_ones = jnp.concatenate([v, pad, ones], axis=1)
            qblk = q_ref[rows, kvh * group_w:(kvh + 1) * group_w]
            q = jnp.concatenate([qblk[:, g * HEAD_DIM:(g + 1) * HEAD_DIM]
                                 for g in range(SW_GROUP)], axis=0)
            s = lax.dot_general(q, k, NT_DIMS, preferred_element_type=F32)
            s = jnp.where(mask, s, -jnp.inf)
            m = jnp.maximum(jnp.max(s, axis=1, keepdims=True), sink_col)
            p = jnp.exp(s - jnp.concatenate([m, m], axis=1))
            pv = jnp.dot(p.astype(BF16), v_ones, preferred_element_type=F32)
            denom = pv[:, LANES:] + jnp.exp(sink_col - m)
            o = (pv[:, :HEAD_DIM] / denom[:, :HEAD_DIM]).astype(BF16)
            o_ref[rows, kvh * group_w:(kvh + 1) * group_w] = jnp.concatenate(
                [o[g * WINDOW:(g + 1) * WINDOW] for g in range(SW_GROUP)], axis=1)


def _swa_attention(qs, ksw, vsw, sinks, layer, tq):
    b, s, width = qs.shape
    kv_width = ksw.shape[-1]
    per = tq // WINDOW
    row = lambda bi, i: (bi, i, 0)
    prev = lambda bi, i: (bi, jnp.maximum(i * per - 1, 0), 0)
    return pl.pallas_call(
        functools.partial(_swa_kernel, tq=tq),
        grid=(b, s // tq),
        in_specs=[pl.BlockSpec((None, tq, width), row),
                  pl.BlockSpec((None, tq, kv_width), row),
                  pl.BlockSpec((None, tq, kv_width), row),
                  pl.BlockSpec((None, WINDOW, kv_width), prev),
                  pl.BlockSpec((None, WINDOW, kv_width), prev),
                  pl.BlockSpec((None, 1, SW_Q_HEADS), lambda bi, i: (layer, 0, 0))],
        out_specs=pl.BlockSpec((None, tq, width), row),
        out_shape=jax.ShapeDtypeStruct((b, s, width), BF16),
        compiler_params=_params(2),
        name="swa_attention",
    )(qs, ksw, vsw, ksw, vsw, sinks)


def _merge_kernel(x_ref, oa_ref, ob_ref, gate_ref, mod_ref, wa_ref, wb_ref, wo_ref, o_ref):
    d = x_ref.shape[-1]
    a = jnp.dot(oa_ref[...], wa_ref[...], preferred_element_type=F32)
    bb = jnp.dot(ob_ref[...], wb_ref[...], preferred_element_type=F32)
    mixed = gate_ref[:, :d].astype(F32) * a + gate_ref[:, d:].astype(F32) * bb
    y = jnp.dot(mixed.astype(BF16), wo_ref[...], preferred_element_type=F32)
    o_ref[...] = x_ref[...] + mod_ref[2:3, :] * y


def _merge_out(x, oa, ob, gates, mod_l, wa_bf, wb_bf, wo_bf, layer, tm):
    b, s, d = x.shape
    row = lambda bi, i: (bi, i, 0)
    wspec = lambda w: _resident((None,) + w.shape[1:], lambda bi, i: (layer, 0, 0))
    return pl.pallas_call(
        _merge_kernel,
        grid=(b, s // tm),
        in_specs=[pl.BlockSpec((None, tm, d), row),
                  pl.BlockSpec((None, tm, oa.shape[-1]), row),
                  pl.BlockSpec((None, tm, ob.shape[-1]), row),
                  pl.BlockSpec((None, tm, 2 * d), row),
                  pl.BlockSpec((None, N_MOD, d), lambda bi, i: (bi, 0, 0)),
                  wspec(wa_bf), wspec(wb_bf), wspec(wo_bf)],
        out_specs=pl.BlockSpec((None, tm, d), row),
        out_shape=jax.ShapeDtypeStruct((b, s, d), F32),
        compiler_params=_params(2),
        name="merge_out",
    )(x, oa, ob, gates, mod_l, wa_bf, wb_bf, wo_bf)


def _ffn_kernel(x_ref, halo_ref, mod_ref, g_ref, wg_ref, wu_ref, cw_ref, cb_ref, wd_ref, o_ref,
                h_ref, act_ref, *, chunk):
    i = pl.program_id(1)
    mod = mod_ref[...]
    shift, scale, gate = mod[3:4], mod[4:5], mod[5:6]
    gain = g_ref[...]

    def normed(rows):
        return _rms(rows, NORM_EPS) * gain * (1.0 + scale) + shift

    x = x_ref[...]
    halo = jnp.where(i > 0, normed(halo_ref[...]), 0.0)
    h_ref[0:BF16_ROWS, :] = halo.astype(BF16)
    h_ref[BF16_ROWS:, :] = normed(x).astype(BF16)

    d_ff = wg_ref.shape[-1]
    for c0 in range(0, d_ff, chunk):
        cols = slice(c0, c0 + chunk)
        g = jnp.dot(h_ref[...], wg_ref[:, cols], preferred_element_type=F32)
        u = jnp.dot(h_ref[BF16_ROWS:, :], wu_ref[:, cols], preferred_element_type=F32)
        gc = cb_ref[:, cols] + cw_ref[CONV_W - 1:CONV_W, cols] * g[BF16_ROWS:]
        for t in range(CONV_W - 1):
            back = CONV_W - 1 - t
            gc = gc + cw_ref[t:t + 1, cols] * pltpu.roll(g, back, axis=0)[BF16_ROWS:]
        act_ref[:, cols] = ((gc * jax.nn.sigmoid(gc)) * u).astype(BF16)

    y = jnp.dot(act_ref[...], wd_ref[...], preferred_element_type=F32)
    o_ref[...] = x + gate * y


def _conv_ffn(x, mod_l, g, wg_bf, wu_bf, conv_w, conv_b, wd_bf, layer, tm):
    b, s, d = x.shape
    d_ff = wg_bf.shape[-1]
    per = tm // BF16_ROWS
    row = lambda bi, i: (bi, i, 0)
    wspec = lambda w: _resident((None,) + w.shape[1:], lambda bi, i: (layer, 0, 0))
    return pl.pallas_call(
        functools.partial(_ffn_kernel, chunk=_ffn_chunk_width(d_ff)),
        grid=(b, s // tm),
        in_specs=[pl.BlockSpec((None, tm, d), row),
                  pl.BlockSpec((None, BF16_ROWS, d),
                               lambda bi, i: (bi, jnp.maximum(i * per - 1, 0), 0)),
                  pl.BlockSpec((None, N_MOD, d), lambda bi, i: (bi, 0, 0)),
                  pl.BlockSpec((None, 1, d), lambda bi, i: (layer, 0, 0)),
                  wspec(wg_bf), wspec(wu_bf), wspec(conv_w), wspec(conv_b), wspec(wd_bf)],
        out_specs=pl.BlockSpec((None, tm, d), row),
        out_shape=jax.ShapeDtypeStruct((b, s, d), F32),
        scratch_shapes=[pltpu.VMEM((BF16_ROWS + tm, d), BF16), pltpu.VMEM((tm, d_ff), BF16)],
        compiler_params=_params(2),
        name="conv_ffn",
    )(x, x, mod_l, g, wg_bf, wu_bf, conv_w, conv_b, wd_bf)


def _final_norm_kernel(x_ref, g_ref, o_ref):
    o_ref[...] = _rms(x_ref[...], NORM_EPS) * g_ref[...]


def _final_norm(x, g, tm):
    b, s, d = x.shape
    row = lambda bi, i: (bi, i, 0)
    return pl.pallas_call(
        _final_norm_kernel,
        grid=(b, s // tm),
        in_specs=[pl.BlockSpec((None, tm, d), row), pl.BlockSpec((1, d), lambda bi, i: (0, 0))],
        out_specs=pl.BlockSpec((None, tm, d), row),
        out_shape=jax.ShapeDtypeStruct((b, s, d), F32),
        compiler_params=_params(2),
        name="final_norm",
    )(x, g.reshape(1, d))


def _ffn_chunk_width(d_ff):
    for cw in (512, 384, 256, 128):
        if d_ff % cw == 0:
            return cw
    raise ValueError(f"d_ff={d_ff} is not a multiple of {LANES}")


def kernel(x, c, positions, w_ada, b_ada, attn_norm, w_in, lambda_qk, subln_norm, sinks,
           w_branch_a, w_branch_b, w_out, ffn_norm, w_gate, w_up, conv_w, conv_b, w_down,
           final_norm):
    b, s, d = x.shape
    depth = w_in.shape[0]
    d_ff = w_gate.shape[-1]
    tm = min(s, 512)
    tq_da = min(s, 512)
    tk_da = tq_da
    tq_sw = min(s, 256)

    mod = _adaln_mod(c, w_ada, b_ada)
    cos, sin_signed = _rope_tables(positions)

    w_in_bf = w_in.astype(BF16)
    wa_bf, wb_bf, wo_bf = (w.astype(BF16) for w in (w_branch_a, w_branch_b, w_out))
    wg_bf, wu_bf, wd_bf = (w.astype(BF16) for w in (w_gate, w_up, w_down))
    conv_b3 = conv_b.reshape(depth, 1, d_ff)
    attn_g = attn_norm.reshape(depth, 1, d)
    ffn_g = ffn_norm.reshape(depth, 1, d)
    subln = subln_norm.reshape(depth, 1, DA_VDIM)
    sinks3 = sinks.reshape(depth, 1, SW_Q_HEADS)

    for l in range(depth):
        lambda_init = 0.8 - 0.6 * math.exp(-0.3 * l)
        mod_l = mod[l]
        qa, ka, va, qs, ksw, vsw, gates = _in_projection(x, mod_l, attn_g, cos, sin_signed,
                                                        w_in_bf, l, tm)
        oa = _diff_attention(qa, ka, va, lambda_qk, subln, l, lambda_init, tq_da, tk_da)
        ob = _swa_attention(qs, ksw, vsw, sinks3, l, tq_sw)
        x = _merge_out(x, oa, ob, gates, mod_l, wa_bf, wb_bf, wo_bf, l, tm)
        x = _conv_ffn(x, mod_l, ffn_g, wg_bf, wu_bf, conv_w, conv_b3, wd_bf, l, tm)
    return _final_norm(x, final_norm, tm)
```

```python
import functools
import math

import jax
import jax.numpy as jnp
from jax import lax
from jax.experimental import pallas as pl
from jax.experimental.pallas import tpu as pltpu

F32 = jnp.float32
BF16 = jnp.bfloat16

HEAD_DIM = 64
DA_HEADS = 8
DA_VDIM = 2 * HEAD_DIM
SW_Q_HEADS = 16
SW_KV_HEADS = 4
SW_GROUP = SW_Q_HEADS // SW_KV_HEADS
WINDOW = 128
CONV_W = 3
ROPE_THETA = 10000.0
NORM_EPS = 1e-6
SUBLN_EPS = 1e-5
N_MOD = 6
QK_SCALE = HEAD_DIM ** -0.5
LOG2_E = math.log2(math.e)

LANES = 128
BF16_ROWS = 16
VMEM_LIMIT = 56 * 1024 * 1024

NT_DIMS = (((1,), (1,)), ((), ()))


def _params(n_grid):
    return pltpu.CompilerParams(dimension_semantics=("arbitrary",) * n_grid,
                                vmem_limit_bytes=VMEM_LIMIT)


def _resident(shape, index_map):
    return pl.BlockSpec(shape, index_map, pipeline_mode=pl.Buffered(1))


def _rms(x, eps):
    return x * lax.rsqrt(jnp.mean(x * x, axis=-1, keepdims=True) + eps)


def _mod_kernel(c_ref, w_ref, b_ref, o_ref):
    c = c_ref[...]
    ca = c * jax.nn.sigmoid(c)
    o_ref[...] = jnp.dot(ca.astype(BF16), w_ref[...].astype(BF16),
                         preferred_element_type=F32) + b_ref[...]


def _adaln_mod(c, w_ada, b_ada):
    depth, d, width = w_ada.shape
    b = c.shape[0]
    rows = -(-b // 8) * 8
    c_pad = jnp.zeros((rows, d), F32).at[:b].set(c)
    tn = 1536
    out = pl.pallas_call(
        _mod_kernel,
        grid=(depth, width // tn),
        in_specs=[pl.BlockSpec((rows, d), lambda l, n: (0, 0)),
                  pl.BlockSpec((None, d, tn), lambda l, n: (l, 0, n)),
                  pl.BlockSpec((None, 1, tn), lambda l, n: (l, 0, n))],
        out_specs=pl.BlockSpec((None, rows, tn), lambda l, n: (l, 0, n)),
        out_shape=jax.ShapeDtypeStruct((depth, rows, width), F32),
        compiler_params=_params(2),
        name="adaln_mod",
    )(c_pad, w_ada, b_ada.reshape(depth, 1, width))
    return out[:, :b].reshape(depth, b, N_MOD, d)


def _rope_kernel(pos_ref, invf_ref, sign_ref, cos_ref, sin_ref):
    ang = pos_ref[...].astype(F32) * invf_ref[...]
    cos_ref[...] = jnp.cos(ang)
    sin_ref[...] = jnp.sin(ang) * sign_ref[...]


def _rope_tables(positions):
    b, s = positions.shape
    inv_freq = ROPE_THETA ** (-jnp.arange(0, HEAD_DIM, 2, dtype=F32) / HEAD_DIM)
    reps = LANES // (HEAD_DIM // 2)
    invf = jnp.tile(inv_freq, reps).reshape(1, LANES)
    half = HEAD_DIM // 2
    sign = jnp.tile(jnp.concatenate([-jnp.ones((half,), F32), jnp.ones((half,), F32)]),
                    LANES // HEAD_DIM).reshape(1, LANES)
    tm = min(s, 1024)
    return pl.pallas_call(
        _rope_kernel,
        grid=(b, s // tm),
        in_specs=[pl.BlockSpec((None, tm, 1), lambda bi, i: (bi, i, 0)),
                  pl.BlockSpec((1, LANES), lambda bi, i: (0, 0)),
                  pl.BlockSpec((1, LANES), lambda bi, i: (0, 0))],
        out_specs=[pl.BlockSpec((None, tm, LANES), lambda bi, i: (bi, i, 0))] * 2,
        out_shape=[jax.ShapeDtypeStruct((b, s, LANES), F32)] * 2,
        compiler_params=_params(2),
        name="rope_tables",
    )(positions.reshape(b, s, 1), invf, sign)


def _rope_block(a, cos, sin_signed, first_half):
    swapped = jnp.where(first_half, pltpu.roll(a, LANES - HEAD_DIM // 2, axis=1),
                        pltpu.roll(a, HEAD_DIM // 2, axis=1))
    return a * cos + swapped * sin_signed


def _inproj_kernel(x_ref, mod_ref, g_ref, cos_ref, sin_ref, w_ref,
                   qa_ref, ka_ref, va_ref, qs_ref, ks_ref, vs_ref, gate_ref, h_ref,
                   *, sections, chunk):
    x = x_ref[...]
    mod = mod_ref[...]
    shift, scale = mod[0:1], mod[1:2]
    h = _rms(x, NORM_EPS) * g_ref[...] * (1.0 + scale) + shift
    h_ref[...] = h.astype(BF16)

    cos = cos_ref[...]
    sin_signed = sin_ref[...]
    lane = lax.broadcasted_iota(jnp.int32, cos.shape, 1)
    first_half = (lane % HEAD_DIM) < (HEAD_DIM // 2)

    outs = (qa_ref, ka_ref, va_ref, qs_ref, ks_ref, vs_ref, gate_ref)
    col = 0
    for out_ref, (width, kind, mult) in zip(outs, sections):
        for c0 in range(0, width, chunk):
            cw = min(chunk, width - c0)
            acc = jnp.dot(h_ref[...], w_ref[:, col + c0:col + c0 + cw],
                          preferred_element_type=F32)
            if kind == "rope":
                for j in range(cw // LANES):
                    blk = _rope_block(acc[:, j * LANES:(j + 1) * LANES], cos, sin_signed,
                                      first_half)
                    if mult != 1.0:
                        blk = blk * mult
                    out_ref[:, c0 + j * LANES:c0 + (j + 1) * LANES] = blk.astype(BF16)
            elif kind == "sigmoid":
                out_ref[:, c0:c0 + cw] = jax.nn.sigmoid(acc).astype(BF16)
            else:
                out_ref[:, c0:c0 + cw] = acc.astype(BF16)
        col += width


def _in_projection(x, mod_l, g, cos, sin_signed, w_in_bf, layer, tm):
    b, s, d = x.shape
    da_qk = DA_HEADS * 2 * HEAD_DIM
    da_v = DA_HEADS * DA_VDIM
    sw_q = SW_Q_HEADS * HEAD_DIM
    sw_kv = SW_KV_HEADS * HEAD_DIM
    sections = ((da_qk, "rope", QK_SCALE * LOG2_E), (da_qk, "rope", 1.0), (da_v, "plain", 1.0),
                (sw_q, "rope", QK_SCALE), (sw_kv, "rope", 1.0), (sw_kv, "plain", 1.0),
                (2 * d, "sigmoid", 1.0))
    widths = [w for w, _, _ in sections]
    in_width = sum(widths)
    assert w_in_bf.shape[1:] == (d, in_width)
    row = lambda bi, i: (bi, i, 0)
    return pl.pallas_call(
        functools.partial(_inproj_kernel, sections=sections, chunk=512),
        grid=(b, s // tm),
        in_specs=[pl.BlockSpec((None, tm, d), row),
                  pl.BlockSpec((None, N_MOD, d), lambda bi, i: (bi, 0, 0)),
                  pl.BlockSpec((None, 1, d), lambda bi, i: (layer, 0, 0)),
                  pl.BlockSpec((None, tm, LANES), row),
                  pl.BlockSpec((None, tm, LANES), row),
                  _resident((None, d, in_width), lambda bi, i: (layer, 0, 0))],
        out_specs=[pl.BlockSpec((None, tm, w), row) for w in widths],
        out_shape=[jax.ShapeDtypeStruct((b, s, w), BF16) for w in widths],
        scratch_shapes=[pltpu.VMEM((tm, d), BF16)],
        compiler_params=_params(2),
        name="in_projection",
    )(x, mod_l, g, cos, sin_signed, w_in_bf)


def _softmax_step(s_t, v_t, m_ref, acc_ref):
    m_old = m_ref[...]
    m_new = jnp.maximum(m_old, jnp.max(s_t, axis=0, keepdims=True))
    alpha = jnp.exp2(m_old - m_new)
    p_t = jnp.exp2(s_t - m_new)
    acc_ref[...] = alpha * acc_ref[...] + jnp.dot(v_t, p_t.astype(BF16),
                                                  preferred_element_type=F32)
    m_ref[...] = m_new


def _diffattn_kernel(q_ref, k_ref, v_ref, lam_ref, g_ref, o_ref,
                     vt_ref, m0_ref, a0_ref, m1_ref, a1_ref, sa_ref, sb_ref,
                     *, lambda_init, tq):
    i = pl.program_id(2)

    @pl.when(i == 0)
    def _():
        def transpose_block(j, carry):
            v = v_ref[pl.ds(pl.multiple_of(j * tq, tq), tq), :].astype(F32)
            vt_ref[j, 0:DA_VDIM, :] = v.T.astype(BF16)
            vt_ref[j, DA_VDIM:, :] = jnp.ones((BF16_ROWS, tq), BF16)
            return carry
        lax.fori_loop(0, vt_ref.shape[0], transpose_block, 0)

    q = q_ref[...]
    lane = lax.broadcasted_iota(jnp.int32, q.shape, 1)
    zero = jnp.zeros_like(q)
    q0 = jnp.where(lane < HEAD_DIM, q, zero)
    q1 = jnp.where(lane >= HEAD_DIM, q, zero)

    for m_ref, a_ref in ((m0_ref, a0_ref), (m1_ref, a1_ref)):
        m_ref[...] = jnp.full(m_ref.shape, -jnp.inf, F32)
        a_ref[...] = jnp.zeros(a_ref.shape, F32)

    def put_scores(s_ref, j):
        k = k_ref[pl.ds(pl.multiple_of(j * tq, tq), tq), :]
        s_ref[0] = lax.dot_general(k, q0, NT_DIMS, preferred_element_type=F32)
        s_ref[1] = lax.dot_general(k, q1, NT_DIMS, preferred_element_type=F32)

    def consume(s_ref, j, masked):
        s0 = s_ref[0]
        s1 = s_ref[1]
        if masked:
            key = lax.broadcasted_iota(jnp.int32, s0.shape, 0)
            query = lax.broadcasted_iota(jnp.int32, s0.shape, 1)
            keep = key <= query
            s0 = jnp.where(keep, s0, -jnp.inf)
            s1 = jnp.where(keep, s1, -jnp.inf)
        v_t = vt_ref[j]
        _softmax_step(s0, v_t, m0_ref, a0_ref)
        _softmax_step(s1, v_t, m1_ref, a1_ref)

    put_scores(sa_ref, 0)

    def pair(t, carry):
        j = 2 * t
        put_scores(sb_ref, j + 1)
        consume(sa_ref, j, masked=False)
        put_scores(sa_ref, j + 2)
        consume(sb_ref, j + 1, masked=False)
        return carry

    lax.fori_loop(0, i // 2, pair, 0)

    @pl.when(i % 2 == 0)
    def _():
        consume(sa_ref, i, masked=True)

    @pl.when(i % 2 == 1)
    def _():
        put_scores(sb_ref, i)
        consume(sa_ref, i - 1, masked=False)
        consume(sb_ref, i, masked=True)

    lq = lam_ref[...]
    lam = (jnp.exp(jnp.sum(lq[0:1] * lq[1:2], axis=1, keepdims=True))
           - jnp.exp(jnp.sum(lq[2:3] * lq[3:4], axis=1, keepdims=True)) + lambda_init)
    a0 = a0_ref[...]
    a1 = a1_ref[...]
    o_t = (a0[:DA_VDIM] / a0[DA_VDIM:DA_VDIM + 1]
           - lam * (a1[:DA_VDIM] / a1[DA_VDIM:DA_VDIM + 1]))
    o_t = o_t * lax.rsqrt(jnp.mean(o_t * o_t, axis=0, keepdims=True) + SUBLN_EPS)
    o_t = o_t * g_ref[...] * (1.0 - lambda_init)
    o_ref[...] = o_t.T.astype(BF16)


def _diff_attention(qa, ka, va, lambda_qk, subln_col, layer, lambda_init, tq):
    b, s, width = qa.shape
    heads = width // DA_VDIM
    assert DA_VDIM == LANES and s % tq == 0
    kv_spec = pl.BlockSpec((None, s, DA_VDIM), lambda bi, h, i: (bi, 0, h))
    stat = pltpu.VMEM((1, tq), F32)
    acc = pltpu.VMEM((DA_VDIM + BF16_ROWS, tq), F32)
    scores = pltpu.VMEM((2, tq, tq), F32)
    return pl.pallas_call(
        functools.partial(_diffattn_kernel, lambda_init=lambda_init, tq=tq),
        grid=(b, heads, s // tq),
        in_specs=[pl.BlockSpec((None, tq, DA_VDIM), lambda bi, h, i: (bi, i, h)),
                  kv_spec, kv_spec,
                  pl.BlockSpec((None, 4, HEAD_DIM), lambda bi, h, i: (layer, 0, 0)),
                  pl.BlockSpec((None, DA_VDIM, 1), lambda bi, h, i: (layer, 0, 0))],
        out_specs=pl.BlockSpec((None, tq, DA_VDIM), lambda bi, h, i: (bi, i, h)),
        out_shape=jax.ShapeDtypeStruct((b, s, width), BF16),
        scratch_shapes=[pltpu.VMEM((s // tq, DA_VDIM + BF16_ROWS, tq), BF16),
                        stat, acc, stat, acc, scores, scores],
        compiler_params=_params(3),
        name="diff_attention",
    )(qa, ka, va, lambda_qk, subln_col)


def _swa_kernel(q_ref, kc_ref, vc_ref, kp_ref, vp_ref, sink_ref, o_ref, *, tq):
    i = pl.program_id(1)
    kcat = jnp.concatenate([kp_ref[...], kc_ref[...]], axis=0)
    vcat = jnp.concatenate([vp_ref[...], vc_ref[...]], axis=0)
    stacked = (SW_GROUP * WINDOW, 2 * WINDOW)
    qi = lax.broadcasted_iota(jnp.int32, stacked, 0) % WINDOW
    kj = lax.broadcasted_iota(jnp.int32, stacked, 1)
    band = (kj > qi) & (kj <= qi + WINDOW)
    band_first = band & (kj >= jnp.where(i > 0, 0, WINDOW))
    sinks = sink_ref[...]
    ones = jnp.ones((2 * WINDOW, LANES), BF16)
    pad = jnp.zeros((2 * WINDOW, LANES - HEAD_DIM), BF16)
    group_w = SW_GROUP * HEAD_DIM
    for kvh in range(SW_KV_HEADS):
        cols = slice(kvh * HEAD_DIM, (kvh + 1) * HEAD_DIM)
        sink_col = jnp.concatenate(
            [jnp.broadcast_to(sinks[:, kvh * SW_GROUP + g:kvh * SW_GROUP + g + 1], (WINDOW, LANES))
             for g in range(SW_GROUP)], axis=0)
        for r in range(tq // WINDOW):
            mask = band_first if r == 0 else band
            rows = slice(r * WINDOW, (r + 1) * WINDOW)
            k = kcat[r * WINDOW:(r + 2) * WINDOW, cols]
            v = vcat[r * WINDOW:(r + 2) * WINDOW, cols]
            v_ones = jnp.concatenate([v, pad, ones], axis=1)
            qblk = q_ref[rows, kvh * group_w:(kvh + 1) * group_w]
            q = jnp.concatenate([qblk[:, g * HEAD_DIM:(g + 1) * HEAD_DIM]
                                 for g in range(SW_GROUP)], axis=0)
            s = lax.dot_general(q, k, NT_DIMS, preferred_element_type=F32)
            s = jnp.where(mask, s, -jnp.inf)
            m = jnp.maximum(jnp.max(s, axis=1, keepdims=True), sink_col)
            p = jnp.exp(s - jnp.concatenate([m, m], axis=1))
            pv = jnp.dot(p.astype(BF16), v_ones, preferred_element_type=F32)
            denom = pv[:, LANES:] + jnp.exp(sink_col - m)
            o = (pv[:, :HEAD_DIM] / denom[:, :HEAD_DIM]).astype(BF16)
            o_ref[rows, kvh * group_w:(kvh + 1) * group_w] = jnp.concatenate(
                [o[g * WINDOW:(g + 1) * WINDOW] for g in range(SW_GROUP)], axis=1)


def _swa_attention(qs, ksw, vsw, sinks, layer, tq):
    b, s, width = qs.shape
    kv_width = ksw.shape[-1]
    per = tq // WINDOW
    row = lambda bi, i: (bi, i, 0)
    prev = lambda bi, i: (bi, jnp.maximum(i * per - 1, 0), 0)
    return pl.pallas_call(
        functools.partial(_swa_kernel, tq=tq),
        grid=(b, s // tq),
        in_specs=[pl.BlockSpec((None, tq, width), row),
                  pl.BlockSpec((None, tq, kv_width), row),
                  pl.BlockSpec((None, tq, kv_width), row),
                  pl.BlockSpec((None, WINDOW, kv_width), prev),
                  pl.BlockSpec((None, WINDOW, kv_width), prev),
                  pl.BlockSpec((None, 1, SW_Q_HEADS), lambda bi, i: (layer, 0, 0))],
        out_specs=pl.BlockSpec((None, tq, width), row),
        out_shape=jax.ShapeDtypeStruct((b, s, width), BF16),
        compiler_params=_params(2),
        name="swa_attention",
    )(qs, ksw, vsw, ksw, vsw, sinks)


def _merge_kernel(x_ref, oa_ref, ob_ref, gate_ref, mod_ref, wa_ref, wb_ref, wo_ref, o_ref):
    d = x_ref.shape[-1]
    a = jnp.dot(oa_ref[...], wa_ref[...], preferred_element_type=F32)
    bb = jnp.dot(ob_ref[...], wb_ref[...], preferred_element_type=F32)
    mixed = gate_ref[:, :d].astype(F32) * a + gate_ref[:, d:].astype(F32) * bb
    y = jnp.dot(mixed.astype(BF16), wo_ref[...], preferred_element_type=F32)
    o_ref[...] = x_ref[...] + mod_ref[2:3, :] * y


def _merge_out(x, oa, ob, gates, mod_l, wa_bf, wb_bf, wo_bf, layer, tm):
    b, s, d = x.shape
    row = lambda bi, i: (bi, i, 0)
    wspec = lambda w: _resident((None,) + w.shape[1:], lambda bi, i: (layer, 0, 0))
    return pl.pallas_call(
        _merge_kernel,
        grid=(b, s // tm),
        in_specs=[pl.BlockSpec((None, tm, d), row),
                  pl.BlockSpec((None, tm, oa.shape[-1]), row),
                  pl.BlockSpec((None, tm, ob.shape[-1]), row),
                  pl.BlockSpec((None, tm, 2 * d), row),
                  pl.BlockSpec((None, N_MOD, d), lambda bi, i: (bi, 0, 0)),
                  wspec(wa_bf), wspec(wb_bf), wspec(wo_bf)],
        out_specs=pl.BlockSpec((None, tm, d), row),
        out_shape=jax.ShapeDtypeStruct((b, s, d), F32),
        compiler_params=_params(2),
        name="merge_out",
    )(x, oa, ob, gates, mod_l, wa_bf, wb_bf, wo_bf)


def _ffn_kernel(x_ref, halo_ref, mod_ref, g_ref, wg_ref, wu_ref, cw_ref, cb_ref, wd_ref, o_ref,
                h_ref, act_ref, *, chunk):
    i = pl.program_id(1)
    mod = mod_ref[...]
    shift, scale, gate = mod[3:4], mod[4:5], mod[5:6]
    gain = g_ref[...]

    def normed(rows):
        return _rms(rows, NORM_EPS) * gain * (1.0 + scale) + shift

    x = x_ref[...]
    halo = jnp.where(i > 0, normed(halo_ref[...]), 0.0)
    h_ref[0:BF16_ROWS, :] = halo.astype(BF16)
    h_ref[BF16_ROWS:, :] = normed(x).astype(BF16)

    d_ff = wg_ref.shape[-1]
    for c0 in range(0, d_ff, chunk):
        cols = slice(c0, c0 + chunk)
        g = jnp.dot(h_ref[...], wg_ref[:, cols], preferred_element_type=F32)
        u = jnp.dot(h_ref[BF16_ROWS:, :], wu_ref[:, cols], preferred_element_type=F32)
        gc = cb_ref[:, cols] + cw_ref[CONV_W - 1:CONV_W, cols] * g[BF16_ROWS:]
        for t in range(CONV_W - 1):
            back = CONV_W - 1 - t
            gc = gc + cw_ref[t:t + 1, cols] * pltpu.roll(g, back, axis=0)[BF16_ROWS:]
        act_ref[:, cols] = ((gc * jax.nn.sigmoid(gc)) * u).astype(BF16)

    y = jnp.dot(act_ref[...], wd_ref[...], preferred_element_type=F32)
    o_ref[...] = x + gate * y


def _conv_ffn(x, mod_l, g, wg_bf, wu_bf, conv_w, conv_b, wd_bf, layer, tm):
    b, s, d = x.shape
    d_ff = wg_bf.shape[-1]
    per = tm // BF16_ROWS
    row = lambda bi, i: (bi, i, 0)
    wspec = lambda w: _resident((None,) + w.shape[1:], lambda bi, i: (layer, 0, 0))
    return pl.pallas_call(
        functools.partial(_ffn_kernel, chunk=_ffn_chunk_width(d_ff)),
        grid=(b, s // tm),
        in_specs=[pl.BlockSpec((None, tm, d), row),
                  pl.BlockSpec((None, BF16_ROWS, d),
                               lambda bi, i: (bi, jnp.maximum(i * per - 1, 0), 0)),
                  pl.BlockSpec((None, N_MOD, d), lambda bi, i: (bi, 0, 0)),
                  pl.BlockSpec((None, 1, d), lambda bi, i: (layer, 0, 0)),
                  wspec(wg_bf), wspec(wu_bf), wspec(conv_w), wspec(conv_b), wspec(wd_bf)],
        out_specs=pl.BlockSpec((None, tm, d), row),
        out_shape=jax.ShapeDtypeStruct((b, s, d), F32),
        scratch_shapes=[pltpu.VMEM((BF16_ROWS + tm, d), BF16), pltpu.VMEM((tm, d_ff), BF16)],
        compiler_params=_params(2),
        name="conv_ffn",
    )(x, x, mod_l, g, wg_bf, wu_bf, conv_w, conv_b, wd_bf)


def _final_norm_kernel(x_ref, g_ref, o_ref):
    o_ref[...] = _rms(x_ref[...], NORM_EPS) * g_ref[...]


def _final_norm(x, g, tm):
    b, s, d = x.shape
    row = lambda bi, i: (bi, i, 0)
    return pl.pallas_call(
        _final_norm_kernel,
        grid=(b, s // tm),
        in_specs=[pl.BlockSpec((None, tm, d), row), pl.BlockSpec((1, d), lambda bi, i: (0, 0))],
        out_specs=pl.BlockSpec((None, tm, d), row),
        out_shape=jax.ShapeDtypeStruct((b, s, d), F32),
        compiler_params=_params(2),
        name="final_norm",
    )(x, g.reshape(1, d))


def _ffn_chunk_width(d_ff):
    for cw in (512, 384, 256, 128):
        if d_ff % cw == 0:
            return cw
    raise ValueError(f"d_ff={d_ff} is not a multiple of {LANES}")


def kernel(x, c, positions, w_ada, b_ada, attn_norm, w_in, lambda_qk, subln_norm, sinks,
           w_branch_a, w_branch_b, w_out, ffn_norm, w_gate, w_up, conv_w, conv_b, w_down,
           final_norm):
    b, s, d = x.shape
    depth = w_in.shape[0]
    d_ff = w_gate.shape[-1]
    tm = min(s, 512)
    tq_da = min(s, 512)
    tq_sw = min(s, 256)

    mod = _adaln_mod(c, w_ada, b_ada)
    cos, sin_signed = _rope_tables(positions)

    w_in_bf = w_in.astype(BF16)
    wa_bf, wb_bf, wo_bf = (w.astype(BF16) for w in (w_branch_a, w_branch_b, w_out))
    wg_bf, wu_bf, wd_bf = (w.astype(BF16) for w in (w_gate, w_up, w_down))
    conv_b3 = conv_b.reshape(depth, 1, d_ff)
    attn_g = attn_norm.reshape(depth, 1, d)
    ffn_g = ffn_norm.reshape(depth, 1, d)
    subln_col = subln_norm.reshape(depth, DA_VDIM, 1)
    sinks3 = sinks.reshape(depth, 1, SW_Q_HEADS)

    for l in range(depth):
        lambda_init = 0.8 - 0.6 * math.exp(-0.3 * l)
        mod_l = mod[l]
        qa, ka, va, qs, ksw, vsw, gates = _in_projection(x, mod_l, attn_g, cos, sin_signed,
                                                        w_in_bf, l, tm)
        oa = _diff_attention(qa, ka, va, lambda_qk, subln_col, l, lambda_init, tq_da)
        ob = _swa_attention(qs, ksw, vsw, sinks3, l, tq_sw)
        x = _merge_out(x, oa, ob, gates, mod_l, wa_bf, wb_bf, wo_bf, l, tm)
        x = _conv_ffn(x, mod_l, ffn_g, wg_bf, wu_bf, conv_w, conv_b3, wd_bf, l, tm)
    return _final_norm(x, final_norm, tm)
```

```python
import functools
import math

import jax
import jax.numpy as jnp
from jax import lax
from jax.experimental import pallas as pl
from jax.experimental.pallas import tpu as pltpu

F32 = jnp.float32
BF16 = jnp.bfloat16

HEAD_DIM = 64
DA_HEADS = 8
DA_VDIM = 2 * HEAD_DIM
SW_Q_HEADS = 16
SW_KV_HEADS = 4
SW_GROUP = SW_Q_HEADS // SW_KV_HEADS
WINDOW = 128
CONV_W = 3
ROPE_THETA = 10000.0
NORM_EPS = 1e-6
SUBLN_EPS = 1e-5
N_MOD = 6
QK_SCALE = HEAD_DIM ** -0.5
LOG2_E = math.log2(math.e)

LANES = 128
BF16_ROWS = 16
VMEM_LIMIT = 56 * 1024 * 1024

NT_DIMS = (((1,), (1,)), ((), ()))


def _params(n_grid):
    return pltpu.CompilerParams(dimension_semantics=("arbitrary",) * n_grid,
                                vmem_limit_bytes=VMEM_LIMIT)


def _resident(shape, index_map):
    return pl.BlockSpec(shape, index_map, pipeline_mode=pl.Buffered(1))


def _rms(x, eps):
    return x * lax.rsqrt(jnp.mean(x * x, axis=-1, keepdims=True) + eps)


def _mod_kernel(c_ref, w_ref, b_ref, o_ref):
    c = c_ref[...]
    ca = c * jax.nn.sigmoid(c)
    o_ref[...] = jnp.dot(ca.astype(BF16), w_ref[...].astype(BF16),
                         preferred_element_type=F32) + b_ref[...]


def _adaln_mod(c, w_ada, b_ada):
    depth, d, width = w_ada.shape
    b = c.shape[0]
    rows = -(-b // 8) * 8
    c_pad = jnp.zeros((rows, d), F32).at[:b].set(c)
    tn = 1536
    out = pl.pallas_call(
        _mod_kernel,
        grid=(depth, width // tn),
        in_specs=[pl.BlockSpec((rows, d), lambda l, n: (0, 0)),
                  pl.BlockSpec((None, d, tn), lambda l, n: (l, 0, n)),
                  pl.BlockSpec((None, 1, tn), lambda l, n: (l, 0, n))],
        out_specs=pl.BlockSpec((None, rows, tn), lambda l, n: (l, 0, n)),
        out_shape=jax.ShapeDtypeStruct((depth, rows, width), F32),
        compiler_params=_params(2),
        name="adaln_mod",
    )(c_pad, w_ada, b_ada.reshape(depth, 1, width))
    return out[:, :b].reshape(depth, b, N_MOD, d)


def _rope_kernel(pos_ref, invf_ref, sign_ref, cos_ref, sin_ref):
    ang = pos_ref[...].astype(F32) * invf_ref[...]
    cos_ref[...] = jnp.cos(ang)
    sin_ref[...] = jnp.sin(ang) * sign_ref[...]


def _rope_tables(positions):
    b, s = positions.shape
    inv_freq = ROPE_THETA ** (-jnp.arange(0, HEAD_DIM, 2, dtype=F32) / HEAD_DIM)
    reps = LANES // (HEAD_DIM // 2)
    invf = jnp.tile(inv_freq, reps).reshape(1, LANES)
    half = HEAD_DIM // 2
    sign = jnp.tile(jnp.concatenate([-jnp.ones((half,), F32), jnp.ones((half,), F32)]),
                    LANES // HEAD_DIM).reshape(1, LANES)
    tm = min(s, 1024)
    return pl.pallas_call(
        _rope_kernel,
        grid=(b, s // tm),
        in_specs=[pl.BlockSpec((None, tm, 1), lambda bi, i: (bi, i, 0)),
                  pl.BlockSpec((1, LANES), lambda bi, i: (0, 0)),
                  pl.BlockSpec((1, LANES), lambda bi, i: (0, 0))],
        out_specs=[pl.BlockSpec((None, tm, LANES), lambda bi, i: (bi, i, 0))] * 2,
        out_shape=[jax.ShapeDtypeStruct((b, s, LANES), F32)] * 2,
        compiler_params=_params(2),
        name="rope_tables",
    )(positions.reshape(b, s, 1), invf, sign)


def _rope_block(a, cos, sin_signed, first_half):
    swapped = jnp.where(first_half, pltpu.roll(a, LANES - HEAD_DIM // 2, axis=1),
                        pltpu.roll(a, HEAD_DIM // 2, axis=1))
    return a * cos + swapped * sin_signed


def _inproj_kernel(x_ref, mod_ref, g_ref, cos_ref, sin_ref, w_ref,
                   qa_ref, ka_ref, va_ref, qs_ref, ks_ref, vs_ref, gate_ref, h_ref,
                   *, sections, chunk):
    x = x_ref[...]
    mod = mod_ref[...]
    shift, scale = mod[0:1], mod[1:2]
    h = _rms(x, NORM_EPS) * g_ref[...] * (1.0 + scale) + shift
    h_ref[...] = h.astype(BF16)

    cos = cos_ref[...]
    sin_signed = sin_ref[...]
    lane = lax.broadcasted_iota(jnp.int32, cos.shape, 1)
    first_half = (lane % HEAD_DIM) < (HEAD_DIM // 2)

    outs = (qa_ref, ka_ref, va_ref, qs_ref, ks_ref, vs_ref, gate_ref)
    col = 0
    for out_ref, (width, kind, mult) in zip(outs, sections):
        for c0 in range(0, width, chunk):
            cw = min(chunk, width - c0)
            acc = jnp.dot(h_ref[...], w_ref[:, col + c0:col + c0 + cw],
                          preferred_element_type=F32)
            if kind == "rope":
                for j in range(cw // LANES):
                    blk = _rope_block(acc[:, j * LANES:(j + 1) * LANES], cos, sin_signed,
                                      first_half)
                    if mult != 1.0:
                        blk = blk * mult
                    out_ref[:, c0 + j * LANES:c0 + (j + 1) * LANES] = blk.astype(BF16)
            elif kind == "sigmoid":
                out_ref[:, c0:c0 + cw] = jax.nn.sigmoid(acc).astype(BF16)
            else:
                out_ref[:, c0:c0 + cw] = acc.astype(BF16)
        col += width


def _in_projection(x, mod_l, g, cos, sin_signed, w_in_bf, layer, tm):
    b, s, d = x.shape
    da_qk = DA_HEADS * 2 * HEAD_DIM
    da_v = DA_HEADS * DA_VDIM
    sw_q = SW_Q_HEADS * HEAD_DIM
    sw_kv = SW_KV_HEADS * HEAD_DIM
    sections = ((da_qk, "rope", QK_SCALE * LOG2_E), (da_qk, "rope", 1.0), (da_v, "plain", 1.0),
                (sw_q, "rope", QK_SCALE), (sw_kv, "rope", 1.0), (sw_kv, "plain", 1.0),
                (2 * d, "sigmoid", 1.0))
    widths = [w for w, _, _ in sections]
    in_width = sum(widths)
    assert w_in_bf.shape[1:] == (d, in_width)
    row = lambda bi, i: (bi, i, 0)
    return pl.pallas_call(
        functools.partial(_inproj_kernel, sections=sections, chunk=512),
        grid=(b, s // tm),
        in_specs=[pl.BlockSpec((None, tm, d), row),
                  pl.BlockSpec((None, N_MOD, d), lambda bi, i: (bi, 0, 0)),
                  pl.BlockSpec((None, 1, d), lambda bi, i: (layer, 0, 0)),
                  pl.BlockSpec((None, tm, LANES), row),
                  pl.BlockSpec((None, tm, LANES), row),
                  _resident((None, d, in_width), lambda bi, i: (layer, 0, 0))],
        out_specs=[pl.BlockSpec((None, tm, w), row) for w in widths],
        out_shape=[jax.ShapeDtypeStruct((b, s, w), BF16) for w in widths],
        scratch_shapes=[pltpu.VMEM((tm, d), BF16)],
        compiler_params=_params(2),
        name="in_projection",
    )(x, mod_l, g, cos, sin_signed, w_in_bf)


def _softmax_step(s_t, v_t, m_ref, acc_ref, off):
    m_old = m_ref[:, off:]
    m_new = jnp.maximum(m_old, jnp.max(s_t, axis=0, keepdims=True))
    alpha = jnp.exp2(m_old - m_new)
    p_t = jnp.exp2(s_t - m_new)
    acc_ref[:, off:] = alpha * acc_ref[:, off:] + jnp.dot(v_t, p_t.astype(BF16),
                                                          preferred_element_type=F32)
    m_ref[:, off:] = m_new


def _diffattn_kernel(q_ref, k_ref, v_ref, lam_ref, g_ref, o_ref,
                     vt_ref, m0_ref, a0_ref, m1_ref, a1_ref, sa_ref, sb_ref,
                     *, lambda_init, tq, tk):
    i = pl.program_id(2)
    diag_blocks = tq // tk

    @pl.when(i == 0)
    def _():
        def transpose_block(j, carry):
            v = v_ref[pl.ds(pl.multiple_of(j * tk, tk), tk), :].astype(F32)
            vt_ref[j, 0:DA_VDIM, :] = v.T.astype(BF16)
            vt_ref[j, DA_VDIM:, :] = jnp.ones((BF16_ROWS, tk), BF16)
            return carry
        lax.fori_loop(0, vt_ref.shape[0], transpose_block, 0)

    q = q_ref[...]
    lane = lax.broadcasted_iota(jnp.int32, q.shape, 1)
    zero = jnp.zeros_like(q)
    q0 = jnp.where(lane < HEAD_DIM, q, zero)
    q1 = jnp.where(lane >= HEAD_DIM, q, zero)

    for m_ref, a_ref in ((m0_ref, a0_ref), (m1_ref, a1_ref)):
        m_ref[...] = jnp.full(m_ref.shape, -jnp.inf, F32)
        a_ref[...] = jnp.zeros(a_ref.shape, F32)

    def put_scores(s_ref, j, off=0):
        k = k_ref[pl.ds(pl.multiple_of(j * tk, tk), tk), :]
        s_ref[0, :, off:] = lax.dot_general(k, q0[off:], NT_DIMS, preferred_element_type=F32)
        s_ref[1, :, off:] = lax.dot_general(k, q1[off:], NT_DIMS, preferred_element_type=F32)

    def consume(s_ref, j, off=0, diagonal=False):
        s0 = s_ref[0, :, off:]
        s1 = s_ref[1, :, off:]
        if diagonal:
            key = lax.broadcasted_iota(jnp.int32, s0.shape, 0)
            query = lax.broadcasted_iota(jnp.int32, s0.shape, 1)
            keep = key <= query
            s0 = jnp.where(keep, s0, -jnp.inf)
            s1 = jnp.where(keep, s1, -jnp.inf)
        v_t = vt_ref[j]
        _softmax_step(s0, v_t, m0_ref, a0_ref, off)
        _softmax_step(s1, v_t, m1_ref, a1_ref, off)

    n_full = i * diag_blocks
    put_scores(sa_ref, 0)

    def pair(t, carry):
        j = 2 * t
        put_scores(sb_ref, j + 1)
        consume(sa_ref, j)
        put_scores(sa_ref, j + 2)
        consume(sb_ref, j + 1)
        return carry

    lax.fori_loop(0, n_full // 2, pair, 0)

    bufs = (sa_ref, sb_ref)
    for d in range(diag_blocks):
        if d + 1 < diag_blocks:
            put_scores(bufs[(d + 1) % 2], n_full + d + 1, (d + 1) * tk)
        consume(bufs[d % 2], n_full + d, d * tk, diagonal=True)

    lq = lam_ref[...]
    lam = (jnp.exp(jnp.sum(lq[0:1] * lq[1:2], axis=1, keepdims=True))
           - jnp.exp(jnp.sum(lq[2:3] * lq[3:4], axis=1, keepdims=True)) + lambda_init)
    a0 = a0_ref[...]
    a1 = a1_ref[...]
    o_t = (a0[:DA_VDIM] / a0[DA_VDIM:DA_VDIM + 1]
           - lam * (a1[:DA_VDIM] / a1[DA_VDIM:DA_VDIM + 1]))
    o_t = o_t * lax.rsqrt(jnp.mean(o_t * o_t, axis=0, keepdims=True) + SUBLN_EPS)
    o_t = o_t * g_ref[...] * (1.0 - lambda_init)
    o_ref[...] = o_t.T.astype(BF16)


def _diff_attention(qa, ka, va, lambda_qk, subln_col, layer, lambda_init, tq, tk):
    b, s, width = qa.shape
    heads = width // DA_VDIM
    assert DA_VDIM == LANES and s % tq == 0 and tq % (2 * tk) == 0
    kv_spec = pl.BlockSpec((None, s, DA_VDIM), lambda bi, h, i: (bi, 0, h))
    stat = pltpu.VMEM((1, tq), F32)
    acc = pltpu.VMEM((DA_VDIM + BF16_ROWS, tq), F32)
    scores = pltpu.VMEM((2, tk, tq), F32)
    return pl.pallas_call(
        functools.partial(_diffattn_kernel, lambda_init=lambda_init, tq=tq, tk=tk),
        grid=(b, heads, s // tq),
        in_specs=[pl.BlockSpec((None, tq, DA_VDIM), lambda bi, h, i: (bi, i, h)),
                  kv_spec, kv_spec,
                  pl.BlockSpec((None, 4, HEAD_DIM), lambda bi, h, i: (layer, 0, 0)),
                  pl.BlockSpec((None, DA_VDIM, 1), lambda bi, h, i: (layer, 0, 0))],
        out_specs=pl.BlockSpec((None, tq, DA_VDIM), lambda bi, h, i: (bi, i, h)),
        out_shape=jax.ShapeDtypeStruct((b, s, width), BF16),
        scratch_shapes=[pltpu.VMEM((s // tk, DA_VDIM + BF16_ROWS, tk), BF16),
                        stat, acc, stat, acc, scores, scores],
        compiler_params=_params(3),
        name="diff_attention",
    )(qa, ka, va, lambda_qk, subln_col)


def _swa_kernel(q_ref, kc_ref, vc_ref, kp_ref, vp_ref, sink_ref, o_ref, *, tq):
    i = pl.program_id(1)
    kcat = jnp.concatenate([kp_ref[...], kc_ref[...]], axis=0)
    vcat = jnp.concatenate([vp_ref[...], vc_ref[...]], axis=0)
    stacked = (SW_GROUP * WINDOW, 2 * WINDOW)
    qi = lax.broadcasted_iota(jnp.int32, stacked, 0) % WINDOW
    kj = lax.broadcasted_iota(jnp.int32, stacked, 1)
    band = (kj > qi) & (kj <= qi + WINDOW)
    band_first = band & (kj >= jnp.where(i > 0, 0, WINDOW))
    sinks = sink_ref[...]
    ones = jnp.ones((2 * WINDOW, LANES), BF16)
    pad = jnp.zeros((2 * WINDOW, LANES - HEAD_DIM), BF16)
    group_w = SW_GROUP * HEAD_DIM
    for kvh in range(SW_KV_HEADS):
        cols = slice(kvh * HEAD_DIM, (kvh + 1) * HEAD_DIM)
        sink_col = jnp.concatenate(
            [jnp.broadcast_to(sinks[:, kvh * SW_GROUP + g:kvh * SW_GROUP + g + 1], (WINDOW, LANES))
             for g in range(SW_GROUP)], axis=0)
        for r in range(tq // WINDOW):
            mask = band_first if r == 0 else band
            rows = slice(r * WINDOW, (r + 1) * WINDOW)
            k = kcat[r * WINDOW:(r + 2) * WINDOW, cols]
            v = vcat[r * WINDOW:(r + 2) * WINDOW, cols]
            v_ones = jnp.concatenate([v, pad, ones], axis=1)
            qblk = q_ref[rows, kvh * group_w:(kvh + 1) * group_w]
            q = jnp.concatenate([qblk[:, g * HEAD_DIM:(g + 1) * HEAD_DIM]
                                 for g in range(SW_GROUP)], axis=0)
            s = lax.dot_general(q, k, NT_DIMS, preferred_element_type=F32)
            s = jnp.where(mask, s, -jnp.inf)
            m = jnp.maximum(jnp.max(s, axis=1, keepdims=True), sink_col)
            p = jnp.exp(s - jnp.concatenate([m, m], axis=1))
            pv = jnp.dot(p.astype(BF16), v_ones, preferred_element_type=F32)
            denom = pv[:, LANES:] + jnp.exp(sink_col - m)
            o = (pv[:, :HEAD_DIM] / denom[:, :HEAD_DIM]).astype(BF16)
            o_ref[rows, kvh * group_w:(kvh + 1) * group_w] = jnp.concatenate(
                [o[g * WINDOW:(g + 1) * WINDOW] for g in range(SW_GROUP)], axis=1)


def _swa_attention(qs, ksw, vsw, sinks, layer, tq):
    b, s, width = qs.shape
    kv_width = ksw.shape[-1]
    per = tq // WINDOW
    row = lambda bi, i: (bi, i, 0)
    prev = lambda bi, i: (bi, jnp.maximum(i * per - 1, 0), 0)
    return pl.pallas_call(
        functools.partial(_swa_kernel, tq=tq),
        grid=(b, s // tq),
        in_specs=[pl.BlockSpec((None, tq, width), row),
                  pl.BlockSpec((None, tq, kv_width), row),
                  pl.BlockSpec((None, tq, kv_width), row),
                  pl.BlockSpec((None, WINDOW, kv_width), prev),
                  pl.BlockSpec((None, WINDOW, kv_width), prev),
                  pl.BlockSpec((None, 1, SW_Q_HEADS), lambda bi, i: (layer, 0, 0))],
        out_specs=pl.BlockSpec((None, tq, width), row),
        out_shape=jax.ShapeDtypeStruct((b, s, width), BF16),
        compiler_params=_params(2),
        name="swa_attention",
    )(qs, ksw, vsw, ksw, vsw, sinks)


def _merge_kernel(x_ref, oa_ref, ob_ref, gate_ref, mod_ref, wa_ref, wb_ref, wo_ref, o_ref):
    d = x_ref.shape[-1]
    a = jnp.dot(oa_ref[...], wa_ref[...], preferred_element_type=F32)
    bb = jnp.dot(ob_ref[...], wb_ref[...], preferred_element_type=F32)
    mixed = gate_ref[:, :d].astype(F32) * a + gate_ref[:, d:].astype(F32) * bb
    y = jnp.dot(mixed.astype(BF16), wo_ref[...], preferred_element_type=F32)
    o_ref[...] = x_ref[...] + mod_ref[2:3, :] * y


def _merge_out(x, oa, ob, gates, mod_l, wa_bf, wb_bf, wo_bf, layer, tm):
    b, s, d = x.shape
    row = lambda bi, i: (bi, i, 0)
    wspec = lambda w: _resident((None,) + w.shape[1:], lambda bi, i: (layer, 0, 0))
    return pl.pallas_call(
        _merge_kernel,
        grid=(b, s // tm),
        in_specs=[pl.BlockSpec((None, tm, d), row),
                  pl.BlockSpec((None, tm, oa.shape[-1]), row),
                  pl.BlockSpec((None, tm, ob.shape[-1]), row),
                  pl.BlockSpec((None, tm, 2 * d), row),
                  pl.BlockSpec((None, N_MOD, d), lambda bi, i: (bi, 0, 0)),
                  wspec(wa_bf), wspec(wb_bf), wspec(wo_bf)],
        out_specs=pl.BlockSpec((None, tm, d), row),
        out_shape=jax.ShapeDtypeStruct((b, s, d), F32),
        compiler_params=_params(2),
        name="merge_out",
    )(x, oa, ob, gates, mod_l, wa_bf, wb_bf, wo_bf)


def _ffn_kernel(x_ref, halo_ref, mod_ref, g_ref, wg_ref, wu_ref, cw_ref, cb_ref, wd_ref, o_ref,
                h_ref, act_ref, *, chunk):
    i = pl.program_id(1)
    mod = mod_ref[...]
    shift, scale, gate = mod[3:4], mod[4:5], mod[5:6]
    gain = g_ref[...]

    def normed(rows):
        return _rms(rows, NORM_EPS) * gain * (1.0 + scale) + shift

    x = x_ref[...]
    halo = jnp.where(i > 0, normed(halo_ref[...]), 0.0)
    h_ref[0:BF16_ROWS, :] = halo.astype(BF16)
    h_ref[BF16_ROWS:, :] = normed(x).astype(BF16)

    d_ff = wg_ref.shape[-1]
    for c0 in range(0, d_ff, chunk):
        cols = slice(c0, c0 + chunk)
        g = jnp.dot(h_ref[...], wg_ref[:, cols], preferred_element_type=F32)
        u = jnp.dot(h_ref[BF16_ROWS:, :], wu_ref[:, cols], preferred_element_type=F32)
        gc = cb_ref[:, cols] + cw_ref[CONV_W - 1:CONV_W, cols] * g[BF16_ROWS:]
        for t in range(CONV_W - 1):
            back = CONV_W - 1 - t
            gc = gc + cw_ref[t:t + 1, cols] * pltpu.roll(g, back, axis=0)[BF16_ROWS:]
        act_ref[:, cols] = ((gc * jax.nn.sigmoid(gc)) * u).astype(BF16)

    y = jnp.dot(act_ref[...], wd_ref[...], preferred_element_type=F32)
    o_ref[...] = x + gate * y


def _conv_ffn(x, mod_l, g, wg_bf, wu_bf, conv_w, conv_b, wd_bf, layer, tm):
    b, s, d = x.shape
    d_ff = wg_bf.shape[-1]
    per = tm // BF16_ROWS
    row = lambda bi, i: (bi, i, 0)
    wspec = lambda w: _resident((None,) + w.shape[1:], lambda bi, i: (layer, 0, 0))
    return pl.pallas_call(
        functools.partial(_ffn_kernel, chunk=_ffn_chunk_width(d_ff)),
        grid=(b, s // tm),
        in_specs=[pl.BlockSpec((None, tm, d), row),
                  pl.BlockSpec((None, BF16_ROWS, d),
                               lambda bi, i: (bi, jnp.maximum(i * per - 1, 0), 0)),
                  pl.BlockSpec((None, N_MOD, d), lambda bi, i: (bi, 0, 0)),
                  pl.BlockSpec((None, 1, d), lambda bi, i: (layer, 0, 0)),
                  wspec(wg_bf), wspec(wu_bf), wspec(conv_w), wspec(conv_b), wspec(wd_bf)],
        out_specs=pl.BlockSpec((None, tm, d), row),
        out_shape=jax.ShapeDtypeStruct((b, s, d), F32),
        scratch_shapes=[pltpu.VMEM((BF16_ROWS + tm, d), BF16), pltpu.VMEM((tm, d_ff), BF16)],
        compiler_params=_params(2),
        name="conv_ffn",
    )(x, x, mod_l, g, wg_bf, wu_bf, conv_w, conv_b, wd_bf)


def _final_norm_kernel(x_ref, g_ref, o_ref):
    o_ref[...] = _rms(x_ref[...], NORM_EPS) * g_ref[...]


def _final_norm(x, g, tm):
    b, s, d = x.shape
    row = lambda bi, i: (bi, i, 0)
    return pl.pallas_call(
        _final_norm_kernel,
        grid=(b, s // tm),
        in_specs=[pl.BlockSpec((None, tm, d), row), pl.BlockSpec((1, d), lambda bi, i: (0, 0))],
        out_specs=pl.BlockSpec((None, tm, d), row),
        out_shape=jax.ShapeDtypeStruct((b, s, d), F32),
        compiler_params=_params(2),
        name="final_norm",
    )(x, g.reshape(1, d))


def _ffn_chunk_width(d_ff):
    for cw in (512, 384, 256, 128):
        if d_ff % cw == 0:
            return cw
    raise ValueError(f"d_ff={d_ff} is not a multiple of {LANES}")


def kernel(x, c, positions, w_ada, b_ada, attn_norm, w_in, lambda_qk, subln_norm, sinks,
           w_branch_a, w_branch_b, w_out, ffn_norm, w_gate, w_up, conv_w, conv_b, w_down,
           final_norm):
    b, s, d = x.shape
    depth = w_in.shape[0]
    d_ff = w_gate.shape[-1]
    tm = min(s, 512)
    tq_da, tk_da = min(s, 2048), 512
    tq_sw = min(s, 512)

    mod = _adaln_mod(c, w_ada, b_ada)
    cos, sin_signed = _rope_tables(positions)

    w_in_bf = w_in.astype(BF16)
    wa_bf, wb_bf, wo_bf = (w.astype(BF16) for w in (w_branch_a, w_branch_b, w_out))
    wg_bf, wu_bf, wd_bf = (w.astype(BF16) for w in (w_gate, w_up, w_down))
    conv_b3 = conv_b.reshape(depth, 1, d_ff)
    attn_g = attn_norm.reshape(depth, 1, d)
    ffn_g = ffn_norm.reshape(depth, 1, d)
    subln_col = subln_norm.reshape(depth, DA_VDIM, 1)
    sinks3 = sinks.reshape(depth, 1, SW_Q_HEADS)

    for l in range(depth):
        lambda_init = 0.8 - 0.6 * math.exp(-0.3 * l)
        mod_l = mod[l]
        qa, ka, va, qs, ksw, vsw, gates = _in_projection(x, mod_l, attn_g, cos, sin_signed,
                                                        w_in_bf, l, tm)
        oa = _diff_attention(qa, ka, va, lambda_qk, subln_col, l, lambda_init, tq_da, tk_da)
        ob = _swa_attention(qs, ksw, vsw, sinks3, l, tq_sw)
        x = _merge_out(x, oa, ob, gates, mod_l, wa_bf, wb_bf, wo_bf, l, tm)
        x = _conv_ffn(x, mod_l, ffn_g, wg_bf, wu_bf, conv_w, conv_b3, wd_bf, l, tm)
    return _final_norm(x, final_norm, tm)
```

```python
import functools
import math

import jax
import jax.numpy as jnp
from jax import lax
from jax.experimental import pallas as pl
from jax.experimental.pallas import tpu as pltpu

F32 = jnp.float32
BF16 = jnp.bfloat16

HEAD_DIM = 64
DA_HEADS = 8
DA_VDIM = 2 * HEAD_DIM
SW_Q_HEADS = 16
SW_KV_HEADS = 4
SW_GROUP = SW_Q_HEADS // SW_KV_HEADS
WINDOW = 128
CONV_W = 3
ROPE_THETA = 10000.0
NORM_EPS = 1e-6
SUBLN_EPS = 1e-5
N_MOD = 6
QK_SCALE = HEAD_DIM ** -0.5
LOG2_E = math.log2(math.e)

LANES = 128
BF16_ROWS = 16
VMEM_LIMIT = 56 * 1024 * 1024

NT_DIMS = (((1,), (1,)), ((), ()))


def _params(n_grid):
    return pltpu.CompilerParams(dimension_semantics=("arbitrary",) * n_grid,
                                vmem_limit_bytes=VMEM_LIMIT)


def _resident(shape, index_map):
    return pl.BlockSpec(shape, index_map, pipeline_mode=pl.Buffered(1))


def _rms(x, eps):
    return x * lax.rsqrt(jnp.mean(x * x, axis=-1, keepdims=True) + eps)


def _mod_kernel(c_ref, w_ref, b_ref, o_ref):
    c = c_ref[...]
    ca = c * jax.nn.sigmoid(c)
    o_ref[...] = jnp.dot(ca.astype(BF16), w_ref[...].astype(BF16),
                         preferred_element_type=F32) + b_ref[...]


def _adaln_mod(c, w_ada, b_ada):
    depth, d, width = w_ada.shape
    b = c.shape[0]
    rows = -(-b // 8) * 8
    c_pad = jnp.zeros((rows, d), F32).at[:b].set(c)
    tn = 1536
    out = pl.pallas_call(
        _mod_kernel,
        grid=(depth, width // tn),
        in_specs=[pl.BlockSpec((rows, d), lambda l, n: (0, 0)),
                  pl.BlockSpec((None, d, tn), lambda l, n: (l, 0, n)),
                  pl.BlockSpec((None, 1, tn), lambda l, n: (l, 0, n))],
        out_specs=pl.BlockSpec((None, rows, tn), lambda l, n: (l, 0, n)),
        out_shape=jax.ShapeDtypeStruct((depth, rows, width), F32),
        compiler_params=_params(2),
        name="adaln_mod",
    )(c_pad, w_ada, b_ada.reshape(depth, 1, width))
    return out[:, :b].reshape(depth, b, N_MOD, d)


def _rope_kernel(pos_ref, invf_ref, sign_ref, cos_ref, sin_ref):
    ang = pos_ref[...].astype(F32) * invf_ref[...]
    cos_ref[...] = jnp.cos(ang)
    sin_ref[...] = jnp.sin(ang) * sign_ref[...]


def _rope_tables(positions):
    b, s = positions.shape
    inv_freq = ROPE_THETA ** (-jnp.arange(0, HEAD_DIM, 2, dtype=F32) / HEAD_DIM)
    reps = LANES // (HEAD_DIM // 2)
    invf = jnp.tile(inv_freq, reps).reshape(1, LANES)
    half = HEAD_DIM // 2
    sign = jnp.tile(jnp.concatenate([-jnp.ones((half,), F32), jnp.ones((half,), F32)]),
                    LANES // HEAD_DIM).reshape(1, LANES)
    tm = min(s, 1024)
    return pl.pallas_call(
        _rope_kernel,
        grid=(b, s // tm),
        in_specs=[pl.BlockSpec((None, tm, 1), lambda bi, i: (bi, i, 0)),
                  pl.BlockSpec((1, LANES), lambda bi, i: (0, 0)),
                  pl.BlockSpec((1, LANES), lambda bi, i: (0, 0))],
        out_specs=[pl.BlockSpec((None, tm, LANES), lambda bi, i: (bi, i, 0))] * 2,
        out_shape=[jax.ShapeDtypeStruct((b, s, LANES), F32)] * 2,
        compiler_params=_params(2),
        name="rope_tables",
    )(positions.reshape(b, s, 1), invf, sign)


def _rope_block(a, cos, sin_signed, first_half):
    swapped = jnp.where(first_half, pltpu.roll(a, LANES - HEAD_DIM // 2, axis=1),
                        pltpu.roll(a, HEAD_DIM // 2, axis=1))
    return a * cos + swapped * sin_signed


def _inproj_kernel(x_ref, mod_ref, g_ref, cos_ref, sin_ref, w_ref,
                   qa_ref, ka_ref, va_ref, qs_ref, ks_ref, vs_ref, gate_ref, h_ref,
                   *, sections, chunk):
    x = x_ref[...]
    mod = mod_ref[...]
    shift, scale = mod[0:1], mod[1:2]
    h = _rms(x, NORM_EPS) * g_ref[...] * (1.0 + scale) + shift
    h_ref[...] = h.astype(BF16)

    cos = cos_ref[...]
    sin_signed = sin_ref[...]
    lane = lax.broadcasted_iota(jnp.int32, cos.shape, 1)
    first_half = (lane % HEAD_DIM) < (HEAD_DIM // 2)

    outs = (qa_ref, ka_ref, va_ref, qs_ref, ks_ref, vs_ref, gate_ref)
    col = 0
    for out_ref, (width, kind, mult) in zip(outs, sections):
        for c0 in range(0, width, chunk):
            cw = min(chunk, width - c0)
            acc = jnp.dot(h_ref[...], w_ref[:, col + c0:col + c0 + cw],
                          preferred_element_type=F32)
            if kind == "rope":
                for j in range(cw // LANES):
                    blk = _rope_block(acc[:, j * LANES:(j + 1) * LANES], cos, sin_signed,
                                      first_half)
                    if mult != 1.0:
                        blk = blk * mult
                    out_ref[:, c0 + j * LANES:c0 + (j + 1) * LANES] = blk.astype(BF16)
            elif kind == "sigmoid":
                out_ref[:, c0:c0 + cw] = jax.nn.sigmoid(acc).astype(BF16)
            else:
                out_ref[:, c0:c0 + cw] = acc.astype(BF16)
        col += width


def _in_projection(x, mod_l, g, cos, sin_signed, w_in_bf, layer, tm):
    b, s, d = x.shape
    da_qk = DA_HEADS * 2 * HEAD_DIM
    da_v = DA_HEADS * DA_VDIM
    sw_q = SW_Q_HEADS * HEAD_DIM
    sw_kv = SW_KV_HEADS * HEAD_DIM
    sections = ((da_qk, "rope", QK_SCALE * LOG2_E), (da_qk, "rope", 1.0), (da_v, "plain", 1.0),
                (sw_q, "rope", QK_SCALE), (sw_kv, "rope", 1.0), (sw_kv, "plain", 1.0),
                (2 * d, "sigmoid", 1.0))
    widths = [w for w, _, _ in sections]
    in_width = sum(widths)
    assert w_in_bf.shape[1:] == (d, in_width)
    row = lambda bi, i: (bi, i, 0)
    return pl.pallas_call(
        functools.partial(_inproj_kernel, sections=sections, chunk=512),
        grid=(b, s // tm),
        in_specs=[pl.BlockSpec((None, tm, d), row),
                  pl.BlockSpec((None, N_MOD, d), lambda bi, i: (bi, 0, 0)),
                  pl.BlockSpec((None, 1, d), lambda bi, i: (layer, 0, 0)),
                  pl.BlockSpec((None, tm, LANES), row),
                  pl.BlockSpec((None, tm, LANES), row),
                  _resident((None, d, in_width), lambda bi, i: (layer, 0, 0))],
        out_specs=[pl.BlockSpec((None, tm, w), row) for w in widths],
        out_shape=[jax.ShapeDtypeStruct((b, s, w), BF16) for w in widths],
        scratch_shapes=[pltpu.VMEM((tm, d), BF16)],
        compiler_params=_params(2),
        name="in_projection",
    )(x, mod_l, g, cos, sin_signed, w_in_bf)


def _diffattn_kernel(q_ref, k_ref, v_ref, lam_ref, g_ref, o_ref,
                     vt_ref, m0_ref, a0_ref, m1_ref, a1_ref, sa_ref, xa_ref, sb_ref, xb_ref,
                     *, lambda_init, tq, tk, cq):
    i = pl.program_id(2)
    diag_blocks = tq // tk

    @pl.when(i == 0)
    def _():
        def transpose_block(j, carry):
            v = v_ref[pl.ds(pl.multiple_of(j * tk, tk), tk), :].astype(F32)
            vt_ref[j, 0:DA_VDIM, :] = v.T.astype(BF16)
            vt_ref[j, DA_VDIM:, :] = jnp.ones((BF16_ROWS, tk), BF16)
            return carry
        lax.fori_loop(0, vt_ref.shape[0], transpose_block, 0)

    q = q_ref[...]
    lane = lax.broadcasted_iota(jnp.int32, q.shape, 1)
    zero = jnp.zeros_like(q)
    halves = ((jnp.where(lane < HEAD_DIM, q, zero), m0_ref, a0_ref),
              (jnp.where(lane >= HEAD_DIM, q, zero), m1_ref, a1_ref))
    for _, m_ref, a_ref in halves:
        m_ref[...] = jnp.full(m_ref.shape, -jnp.inf, F32)
        a_ref[...] = jnp.zeros(a_ref.shape, F32)

    def produce(buf, j, c0, diag_off):
        s_ref, x_ref = buf
        k = k_ref[pl.ds(pl.multiple_of(j * tk, tk), tk), :]
        for h, (qh, _, _) in enumerate(halves):
            s_t = lax.dot_general(k, qh[c0:c0 + cq], NT_DIMS, preferred_element_type=F32)
            if diag_off is not None:
                key = lax.broadcasted_iota(jnp.int32, s_t.shape, 0)
                query = lax.broadcasted_iota(jnp.int32, s_t.shape, 1) + diag_off
                s_t = jnp.where(key <= query, s_t, -jnp.inf)
            s_ref[h, :, c0:c0 + cq] = s_t
            x_ref[h, :, c0:c0 + cq] = jnp.max(s_t, axis=0, keepdims=True)

    def consume(buf, j, c0):
        s_ref, x_ref = buf
        v_t = vt_ref[j]
        for h, (_, m_ref, a_ref) in enumerate(halves):
            m_old = m_ref[:, c0:c0 + cq]
            m_new = jnp.maximum(m_old, x_ref[h, :, c0:c0 + cq])
            alpha = jnp.exp2(m_old - m_new)
            p_t = jnp.exp2(s_ref[h, :, c0:c0 + cq] - m_new).astype(BF16)
            a_ref[:, c0:c0 + cq] = alpha * a_ref[:, c0:c0 + cq] + jnp.dot(
                v_t, p_t, preferred_element_type=F32)
            m_ref[:, c0:c0 + cq] = m_new

    def chunks(diag):
        if diag is None:
            return [(c0, None) for c0 in range(0, tq, cq)]
        first = diag * tk
        return [(c0, c0 - first if c0 < first + tk else None) for c0 in range(first, tq, cq)]

    def overlap(produced, consumed):
        puts = [] if produced is None else [
            (produced[0], produced[1], c0, off) for c0, off in chunks(produced[2])]
        gets = [] if consumed is None else [
            (consumed[0], consumed[1], c0) for c0, _ in chunks(consumed[2])]
        for n in range(max(len(puts), len(gets))):
            if n < len(puts):
                produce(*puts[n])
            if n < len(gets):
                consume(*gets[n])

    buf_a, buf_b = (sa_ref, xa_ref), (sb_ref, xb_ref)
    n_full = i * diag_blocks

    @pl.when(i == 0)
    def _():
        overlap((buf_a, 0, 0), None)

    @pl.when(i > 0)
    def _():
        overlap((buf_a, 0, None), None)

    def pair(t, carry):
        j = 2 * t
        overlap((buf_b, j + 1, None), (buf_a, j, None))
        overlap((buf_a, j + 2, None), (buf_b, j + 1, None))
        return carry

    lax.fori_loop(0, n_full // 2 - 1, pair, 0)

    @pl.when(i > 0)
    def _():
        j = n_full - 2
        overlap((buf_b, j + 1, None), (buf_a, j, None))
        overlap((buf_a, j + 2, 0), (buf_b, j + 1, None))

    bufs = (buf_a, buf_b)
    for d in range(diag_blocks):
        following = (bufs[(d + 1) % 2], n_full + d + 1, d + 1) if d + 1 < diag_blocks else None
        overlap(following, (bufs[d % 2], n_full + d, d))

    lq = lam_ref[...]
    lam = (jnp.exp(jnp.sum(lq[0:1] * lq[1:2], axis=1, keepdims=True))
           - jnp.exp(jnp.sum(lq[2:3] * lq[3:4], axis=1, keepdims=True)) + lambda_init)
    a0 = a0_ref[...]
    a1 = a1_ref[...]
    o_t = (a0[:DA_VDIM] / a0[DA_VDIM:DA_VDIM + 1]
           - lam * (a1[:DA_VDIM] / a1[DA_VDIM:DA_VDIM + 1]))
    o_t = o_t * lax.rsqrt(jnp.mean(o_t * o_t, axis=0, keepdims=True) + SUBLN_EPS)
    o_t = o_t * g_ref[...] * (1.0 - lambda_init)
    o_ref[...] = o_t.T.astype(BF16)


def _diff_attention(qa, ka, va, lambda_qk, subln_col, layer, lambda_init, tq, tk, cq):
    b, s, width = qa.shape
    heads = width // DA_VDIM
    assert DA_VDIM == LANES and s % tq == 0 and tq % (2 * tk) == 0 and tk % cq == 0
    kv_spec = pl.BlockSpec((None, s, DA_VDIM), lambda bi, h, i: (bi, 0, h))
    stat = pltpu.VMEM((1, tq), F32)
    acc = pltpu.VMEM((DA_VDIM + BF16_ROWS, tq), F32)
    scores = pltpu.VMEM((2, tk, tq), F32)
    block_max = pltpu.VMEM((2, 1, tq), F32)
    return pl.pallas_call(
        functools.partial(_diffattn_kernel, lambda_init=lambda_init, tq=tq, tk=tk, cq=cq),
        grid=(b, heads, s // tq),
        in_specs=[pl.BlockSpec((None, tq, DA_VDIM), lambda bi, h, i: (bi, i, h)),
                  kv_spec, kv_spec,
                  pl.BlockSpec((None, 4, HEAD_DIM), lambda bi, h, i: (layer, 0, 0)),
                  pl.BlockSpec((None, DA_VDIM, 1), lambda bi, h, i: (layer, 0, 0))],
        out_specs=pl.BlockSpec((None, tq, DA_VDIM), lambda bi, h, i: (bi, i, h)),
        out_shape=jax.ShapeDtypeStruct((b, s, width), BF16),
        scratch_shapes=[pltpu.VMEM((s // tk, DA_VDIM + BF16_ROWS, tk), BF16),
                        stat, acc, stat, acc, scores, block_max, scores, block_max],
        compiler_params=_params(3),
        name="diff_attention",
    )(qa, ka, va, lambda_qk, subln_col)


def _swa_kernel(q_ref, kc_ref, vc_ref, kp_ref, vp_ref, sink_ref, o_ref, *, tq):
    i = pl.program_id(1)
    kcat = jnp.concatenate([kp_ref[...], kc_ref[...]], axis=0)
    vcat = jnp.concatenate([vp_ref[...], vc_ref[...]], axis=0)
    n_r = tq // WINDOW
    rows_per = SW_GROUP * WINDOW
    stacked = (n_r * rows_per, 2 * WINDOW)
    row = lax.broadcasted_iota(jnp.int32, stacked, 0)
    qi = row % WINDOW
    kj = lax.broadcasted_iota(jnp.int32, stacked, 1)
    before_start = (row < rows_per) & (i == 0) & (kj < WINDOW)
    keep = (kj > qi) & (kj <= qi + WINDOW) & jnp.logical_not(before_start)
    bias = jnp.where(keep, 0.0, -jnp.inf).astype(F32)
    sinks = sink_ref[...]
    ones = jnp.ones((2 * WINDOW, LANES), BF16)
    pad = jnp.zeros((2 * WINDOW, LANES - HEAD_DIM), BF16)
    group_w = SW_GROUP * HEAD_DIM
    for kvh in range(SW_KV_HEADS):
        cols = slice(kvh * HEAD_DIM, (kvh + 1) * HEAD_DIM)
        sink_blk = jnp.concatenate(
            [jnp.broadcast_to(sinks[:, kvh * SW_GROUP + g:kvh * SW_GROUP + g + 1], (WINDOW, LANES))
             for g in range(SW_GROUP)], axis=0)
        sink_col = jnp.concatenate([sink_blk] * n_r, axis=0)
        scores = []
        for r in range(n_r):
            k = kcat[r * WINDOW:(r + 2) * WINDOW, cols]
            qblk = q_ref[r * WINDOW:(r + 1) * WINDOW, kvh * group_w:(kvh + 1) * group_w]
            q = jnp.concatenate([qblk[:, g * HEAD_DIM:(g + 1) * HEAD_DIM]
                                 for g in range(SW_GROUP)], axis=0)
            scores.append(lax.dot_general(q, k, NT_DIMS, preferred_element_type=F32))
        s = jnp.concatenate(scores, axis=0) + bias
        m = jnp.maximum(jnp.max(s, axis=1, keepdims=True), sink_col)
        p = jnp.exp(s - jnp.concatenate([m, m], axis=1)).astype(BF16)
        pvs = []
        for r in range(n_r):
            v = vcat[r * WINDOW:(r + 2) * WINDOW, cols]
            v_ones = jnp.concatenate([v, pad, ones], axis=1)
            pvs.append(jnp.dot(p[r * rows_per:(r + 1) * rows_per], v_ones,
                               preferred_element_type=F32))
        pv = jnp.concatenate(pvs, axis=0)
        denom = pv[:, LANES:] + jnp.exp(sink_col - m)
        o = (pv[:, :HEAD_DIM] / denom[:, :HEAD_DIM]).astype(BF16)
        for r in range(n_r):
            o_ref[r * WINDOW:(r + 1) * WINDOW, kvh * group_w:(kvh + 1) * group_w] = (
                jnp.concatenate([o[(r * SW_GROUP + g) * WINDOW:(r * SW_GROUP + g + 1) * WINDOW]
                                 for g in range(SW_GROUP)], axis=1))


def _swa_attention(qs, ksw, vsw, sinks, layer, tq):
    b, s, width = qs.shape
    kv_width = ksw.shape[-1]
    per = tq // WINDOW
    row = lambda bi, i: (bi, i, 0)
    prev = lambda bi, i: (bi, jnp.maximum(i * per - 1, 0), 0)
    return pl.pallas_call(
        functools.partial(_swa_kernel, tq=tq),
        grid=(b, s // tq),
        in_specs=[pl.BlockSpec((None, tq, width), row),
                  pl.BlockSpec((None, tq, kv_width), row),
                  pl.BlockSpec((None, tq, kv_width), row),
                  pl.BlockSpec((None, WINDOW, kv_width), prev),
                  pl.BlockSpec((None, WINDOW, kv_width), prev),
                  pl.BlockSpec((None, 1, SW_Q_HEADS), lambda bi, i: (layer, 0, 0))],
        out_specs=pl.BlockSpec((None, tq, width), row),
        out_shape=jax.ShapeDtypeStruct((b, s, width), BF16),
        compiler_params=_params(2),
        name="swa_attention",
    )(qs, ksw, vsw, ksw, vsw, sinks)


def _merge_kernel(x_ref, oa_ref, ob_ref, gate_ref, mod_ref, wa_ref, wb_ref, wo_ref, o_ref):
    d = x_ref.shape[-1]
    a = jnp.dot(oa_ref[...], wa_ref[...], preferred_element_type=F32)
    bb = jnp.dot(ob_ref[...], wb_ref[...], preferred_element_type=F32)
    mixed = gate_ref[:, :d].astype(F32) * a + gate_ref[:, d:].astype(F32) * bb
    y = jnp.dot(mixed.astype(BF16), wo_ref[...], preferred_element_type=F32)
    o_ref[...] = x_ref[...] + mod_ref[2:3, :] * y


def _merge_out(x, oa, ob, gates, mod_l, wa_bf, wb_bf, wo_bf, layer, tm):
    b, s, d = x.shape
    row = lambda bi, i: (bi, i, 0)
    wspec = lambda w: _resident((None,) + w.shape[1:], lambda bi, i: (layer, 0, 0))
    return pl.pallas_call(
        _merge_kernel,
        grid=(b, s // tm),
        in_specs=[pl.BlockSpec((None, tm, d), row),
                  pl.BlockSpec((None, tm, oa.shape[-1]), row),
                  pl.BlockSpec((None, tm, ob.shape[-1]), row),
                  pl.BlockSpec((None, tm, 2 * d), row),
                  pl.BlockSpec((None, N_MOD, d), lambda bi, i: (bi, 0, 0)),
                  wspec(wa_bf), wspec(wb_bf), wspec(wo_bf)],
        out_specs=pl.BlockSpec((None, tm, d), row),
        out_shape=jax.ShapeDtypeStruct((b, s, d), F32),
        compiler_params=_params(2),
        name="merge_out",
    )(x, oa, ob, gates, mod_l, wa_bf, wb_bf, wo_bf)


def _ffn_kernel(x_ref, halo_ref, mod_ref, g_ref, wg_ref, wu_ref, cw_ref, cb_ref, wd_ref, o_ref,
                h_ref, act_ref, *, chunk):
    i = pl.program_id(1)
    mod = mod_ref[...]
    shift, scale, gate = mod[3:4], mod[4:5], mod[5:6]
    gain = g_ref[...]

    def normed(rows):
        return _rms(rows, NORM_EPS) * gain * (1.0 + scale) + shift

    x = x_ref[...]
    halo = jnp.where(i > 0, normed(halo_ref[...]), 0.0)
    h_ref[0:BF16_ROWS, :] = halo.astype(BF16)
    h_ref[BF16_ROWS:, :] = normed(x).astype(BF16)

    d_ff = wg_ref.shape[-1]
    for c0 in range(0, d_ff, chunk):
        cols = slice(c0, c0 + chunk)
        g = jnp.dot(h_ref[...], wg_ref[:, cols], preferred_element_type=F32)
        u = jnp.dot(h_ref[BF16_ROWS:, :], wu_ref[:, cols], preferred_element_type=F32)
        gc = cb_ref[:, cols] + cw_ref[CONV_W - 1:CONV_W, cols] * g[BF16_ROWS:]
        for t in range(CONV_W - 1):
            back = CONV_W - 1 - t
            gc = gc + cw_ref[t:t + 1, cols] * pltpu.roll(g, back, axis=0)[BF16_ROWS:]
        act_ref[:, cols] = ((gc * jax.nn.sigmoid(gc)) * u).astype(BF16)

    y = jnp.dot(act_ref[...], wd_ref[...], preferred_element_type=F32)
    o_ref[...] = x + gate * y


def _conv_ffn(x, mod_l, g, wg_bf, wu_bf, conv_w, conv_b, wd_bf, layer, tm):
    b, s, d = x.shape
    d_ff = wg_bf.shape[-1]
    per = tm // BF16_ROWS
    row = lambda bi, i: (bi, i, 0)
    wspec = lambda w: _resident((None,) + w.shape[1:], lambda bi, i: (layer, 0, 0))
    return pl.pallas_call(
        functools.partial(_ffn_kernel, chunk=_ffn_chunk_width(d_ff)),
        grid=(b, s // tm),
        in_specs=[pl.BlockSpec((None, tm, d), row),
                  pl.BlockSpec((None, BF16_ROWS, d),
                               lambda bi, i: (bi, jnp.maximum(i * per - 1, 0), 0)),
                  pl.BlockSpec((None, N_MOD, d), lambda bi, i: (bi, 0, 0)),
                  pl.BlockSpec((None, 1, d), lambda bi, i: (layer, 0, 0)),
                  wspec(wg_bf), wspec(wu_bf), wspec(conv_w), wspec(conv_b), wspec(wd_bf)],
        out_specs=pl.BlockSpec((None, tm, d), row),
        out_shape=jax.ShapeDtypeStruct((b, s, d), F32),
        scratch_shapes=[pltpu.VMEM((BF16_ROWS + tm, d), BF16), pltpu.VMEM((tm, d_ff), BF16)],
        compiler_params=_params(2),
        name="conv_ffn",
    )(x, x, mod_l, g, wg_bf, wu_bf, conv_w, conv_b, wd_bf)


def _final_norm_kernel(x_ref, g_ref, o_ref):
    o_ref[...] = _rms(x_ref[...], NORM_EPS) * g_ref[...]


def _final_norm(x, g, tm):
    b, s, d = x.shape
    row = lambda bi, i: (bi, i, 0)
    return pl.pallas_call(
        _final_norm_kernel,
        grid=(b, s // tm),
        in_specs=[pl.BlockSpec((None, tm, d), row), pl.BlockSpec((1, d), lambda bi, i: (0, 0))],
        out_specs=pl.BlockSpec((None, tm, d), row),
        out_shape=jax.ShapeDtypeStruct((b, s, d), F32),
        compiler_params=_params(2),
        name="final_norm",
    )(x, g.reshape(1, d))


def _ffn_chunk_width(d_ff):
    for cw in (512, 384, 256, 128):
        if d_ff % cw == 0:
            return cw
    raise ValueError(f"d_ff={d_ff} is not a multiple of {LANES}")


def kernel(x, c, positions, w_ada, b_ada, attn_norm, w_in, lambda_qk, subln_norm, sinks,
           w_branch_a, w_branch_b, w_out, ffn_norm, w_gate, w_up, conv_w, conv_b, w_down,
           final_norm):
    b, s, d = x.shape
    depth = w_in.shape[0]
    d_ff = w_gate.shape[-1]
    tm = min(s, 512)
    tq_da, tk_da, cq_da = min(s, 2048), 512, 256
    tq_sw = min(s, 512)

    mod = _adaln_mod(c, w_ada, b_ada)
    cos, sin_signed = _rope_tables(positions)

    w_in_bf = w_in.astype(BF16)
    wa_bf, wb_bf, wo_bf = (w.astype(BF16) for w in (w_branch_a, w_branch_b, w_out))
    wg_bf, wu_bf, wd_bf = (w.astype(BF16) for w in (w_gate, w_up, w_down))
    conv_b3 = conv_b.reshape(depth, 1, d_ff)
    attn_g = attn_norm.reshape(depth, 1, d)
    ffn_g = ffn_norm.reshape(depth, 1, d)
    subln_col = subln_norm.reshape(depth, DA_VDIM, 1)
    sinks3 = sinks.reshape(depth, 1, SW_Q_HEADS)

    for l in range(depth):
        lambda_init = 0.8 - 0.6 * math.exp(-0.3 * l)
        mod_l = mod[l]
        qa, ka, va, qs, ksw, vsw, gates = _in_projection(x, mod_l, attn_g, cos, sin_signed,
                                                        w_in_bf, l, tm)
        oa = _diff_attention(qa, ka, va, lambda_qk, subln_col, l, lambda_init, tq_da, tk_da,
                             cq_da)
        ob = _swa_attention(qs, ksw, vsw, sinks3, l, tq_sw)
        x = _merge_out(x, oa, ob, gates, mod_l, wa_bf, wb_bf, wo_bf, l, tm)
        x = _conv_ffn(x, mod_l, ffn_g, wg_bf, wu_bf, conv_w, conv_b3, wd_bf, l, tm)
    return _final_norm(x, final_norm, tm)
```

```python
import functools
import math

import jax
import jax.numpy as jnp
from jax import lax
from jax.experimental import pallas as pl
from jax.experimental.pallas import tpu as pltpu

F32 = jnp.float32
BF16 = jnp.bfloat16

HEAD_DIM = 64
DA_HEADS = 8
DA_VDIM = 2 * HEAD_DIM
SW_Q_HEADS = 16
SW_KV_HEADS = 4
SW_GROUP = SW_Q_HEADS // SW_KV_HEADS
WINDOW = 128
CONV_W = 3
ROPE_THETA = 10000.0
NORM_EPS = 1e-6
SUBLN_EPS = 1e-5
N_MOD = 6
QK_SCALE = HEAD_DIM ** -0.5
LOG2_E = math.log2(math.e)

LANES = 128
BF16_ROWS = 16
VMEM_LIMIT = 56 * 1024 * 1024

NT_DIMS = (((1,), (1,)), ((), ()))


def _params(n_grid):
    return pltpu.CompilerParams(dimension_semantics=("arbitrary",) * n_grid,
                                vmem_limit_bytes=VMEM_LIMIT)


def _resident(shape, index_map):
    return pl.BlockSpec(shape, index_map, pipeline_mode=pl.Buffered(1))


def _rms(x, eps):
    return x * lax.rsqrt(jnp.mean(x * x, axis=-1, keepdims=True) + eps)


def _mod_kernel(c_ref, w_ref, b_ref, o_ref):
    c = c_ref[...]
    ca = c * jax.nn.sigmoid(c)
    o_ref[...] = jnp.dot(ca.astype(BF16), w_ref[...].astype(BF16),
                         preferred_element_type=F32) + b_ref[...]


def _adaln_mod(c, w_ada, b_ada):
    depth, d, width = w_ada.shape
    b = c.shape[0]
    rows = -(-b // 8) * 8
    c_pad = jnp.zeros((rows, d), F32).at[:b].set(c)
    tn = 1536
    out = pl.pallas_call(
        _mod_kernel,
        grid=(depth, width // tn),
        in_specs=[pl.BlockSpec((rows, d), lambda l, n: (0, 0)),
                  pl.BlockSpec((None, d, tn), lambda l, n: (l, 0, n)),
                  pl.BlockSpec((None, 1, tn), lambda l, n: (l, 0, n))],
        out_specs=pl.BlockSpec((None, rows, tn), lambda l, n: (l, 0, n)),
        out_shape=jax.ShapeDtypeStruct((depth, rows, width), F32),
        compiler_params=_params(2),
        name="adaln_mod",
    )(c_pad, w_ada, b_ada.reshape(depth, 1, width))
    return out[:, :b].reshape(depth, b, N_MOD, d)


def _rope_kernel(pos_ref, invf_ref, sign_ref, cos_ref, sin_ref):
    ang = pos_ref[...].astype(F32) * invf_ref[...]
    cos_ref[...] = jnp.cos(ang)
    sin_ref[...] = jnp.sin(ang) * sign_ref[...]


def _rope_tables(positions):
    b, s = positions.shape
    inv_freq = ROPE_THETA ** (-jnp.arange(0, HEAD_DIM, 2, dtype=F32) / HEAD_DIM)
    reps = LANES // (HEAD_DIM // 2)
    invf = jnp.tile(inv_freq, reps).reshape(1, LANES)
    half = HEAD_DIM // 2
    sign = jnp.tile(jnp.concatenate([-jnp.ones((half,), F32), jnp.ones((half,), F32)]),
                    LANES // HEAD_DIM).reshape(1, LANES)
    tm = min(s, 1024)
    return pl.pallas_call(
        _rope_kernel,
        grid=(b, s // tm),
        in_specs=[pl.BlockSpec((None, tm, 1), lambda bi, i: (bi, i, 0)),
                  pl.BlockSpec((1, LANES), lambda bi, i: (0, 0)),
                  pl.BlockSpec((1, LANES), lambda bi, i: (0, 0))],
        out_specs=[pl.BlockSpec((None, tm, LANES), lambda bi, i: (bi, i, 0))] * 2,
        out_shape=[jax.ShapeDtypeStruct((b, s, LANES), F32)] * 2,
        compiler_params=_params(2),
        name="rope_tables",
    )(positions.reshape(b, s, 1), invf, sign)


def _rope_block(a, cos, sin_signed, first_half):
    swapped = jnp.where(first_half, pltpu.roll(a, LANES - HEAD_DIM // 2, axis=1),
                        pltpu.roll(a, HEAD_DIM // 2, axis=1))
    return a * cos + swapped * sin_signed


def _inproj_kernel(x_ref, mod_ref, g_ref, cos_ref, sin_ref, w_ref,
                   qa_ref, ka_ref, va_ref, qs_ref, ks_ref, vs_ref, gate_ref, h_ref,
                   *, sections, chunk):
    x = x_ref[...]
    mod = mod_ref[...]
    shift, scale = mod[0:1], mod[1:2]
    h = _rms(x, NORM_EPS) * g_ref[...] * (1.0 + scale) + shift
    h_ref[...] = h.astype(BF16)

    cos = cos_ref[...]
    sin_signed = sin_ref[...]
    lane = lax.broadcasted_iota(jnp.int32, cos.shape, 1)
    first_half = (lane % HEAD_DIM) < (HEAD_DIM // 2)

    outs = (qa_ref, ka_ref, va_ref, qs_ref, ks_ref, vs_ref, gate_ref)
    col = 0
    for out_ref, (width, kind, mult) in zip(outs, sections):
        for c0 in range(0, width, chunk):
            cw = min(chunk, width - c0)
            acc = jnp.dot(h_ref[...], w_ref[:, col + c0:col + c0 + cw],
                          preferred_element_type=F32)
            if kind == "rope":
                for j in range(cw // LANES):
                    blk = _rope_block(acc[:, j * LANES:(j + 1) * LANES], cos, sin_signed,
                                      first_half)
                    if mult != 1.0:
                        blk = blk * mult
                    out_ref[:, c0 + j * LANES:c0 + (j + 1) * LANES] = blk.astype(BF16)
            elif kind == "sigmoid":
                out_ref[:, c0:c0 + cw] = jax.nn.sigmoid(acc).astype(BF16)
            else:
                out_ref[:, c0:c0 + cw] = acc.astype(BF16)
        col += width


def _in_projection(x, mod_l, g, cos, sin_signed, w_in_bf, layer, tm):
    b, s, d = x.shape
    da_qk = DA_HEADS * 2 * HEAD_DIM
    da_v = DA_HEADS * DA_VDIM
    sw_q = SW_Q_HEADS * HEAD_DIM
    sw_kv = SW_KV_HEADS * HEAD_DIM
    sections = ((da_qk, "rope", QK_SCALE * LOG2_E), (da_qk, "rope", 1.0), (da_v, "plain", 1.0),
                (sw_q, "rope", QK_SCALE), (sw_kv, "rope", 1.0), (sw_kv, "plain", 1.0),
                (2 * d, "sigmoid", 1.0))
    widths = [w for w, _, _ in sections]
    in_width = sum(widths)
    assert w_in_bf.shape[1:] == (d, in_width)
    row = lambda bi, i: (bi, i, 0)
    return pl.pallas_call(
        functools.partial(_inproj_kernel, sections=sections, chunk=512),
        grid=(b, s // tm),
        in_specs=[pl.BlockSpec((None, tm, d), row),
                  pl.BlockSpec((None, N_MOD, d), lambda bi, i: (bi, 0, 0)),
                  pl.BlockSpec((None, 1, d), lambda bi, i: (layer, 0, 0)),
                  pl.BlockSpec((None, tm, LANES), row),
                  pl.BlockSpec((None, tm, LANES), row),
                  _resident((None, d, in_width), lambda bi, i: (layer, 0, 0))],
        out_specs=[pl.BlockSpec((None, tm, w), row) for w in widths],
        out_shape=[jax.ShapeDtypeStruct((b, s, w), BF16) for w in widths],
        scratch_shapes=[pltpu.VMEM((tm, d), BF16)],
        compiler_params=_params(2),
        name="in_projection",
    )(x, mod_l, g, cos, sin_signed, w_in_bf)


def _diffattn_kernel(q_ref, k_ref, v_ref, lam_ref, g_ref, o_ref,
                     vt_ref, m0_ref, a0_ref, m1_ref, a1_ref, sa_ref, xa_ref, sb_ref, xb_ref,
                     *, lambda_init, tq, tk, cq):
    i = pl.program_id(2)
    diag_blocks = tq // tk

    @pl.when(i == 0)
    def _():
        def transpose_block(j, carry):
            v = v_ref[pl.ds(pl.multiple_of(j * tk, tk), tk), :].astype(F32)
            vt_ref[j, 0:DA_VDIM, :] = v.T.astype(BF16)
            vt_ref[j, DA_VDIM:, :] = jnp.ones((BF16_ROWS, tk), BF16)
            return carry
        lax.fori_loop(0, vt_ref.shape[0], transpose_block, 0)

    q = q_ref[...]
    lane = lax.broadcasted_iota(jnp.int32, q.shape, 1)
    zero = jnp.zeros_like(q)
    halves = ((jnp.where(lane < HEAD_DIM, q, zero), m0_ref, a0_ref),
              (jnp.where(lane >= HEAD_DIM, q, zero), m1_ref, a1_ref))
    for _, m_ref, a_ref in halves:
        m_ref[...] = jnp.full(m_ref.shape, -jnp.inf, F32)
        a_ref[...] = jnp.zeros(a_ref.shape, F32)

    def live_keys(diag_off):
        return tk if diag_off is None else min(tk, diag_off + cq)

    def produce(buf, j, c0, diag_off, h):
        s_ref, x_ref = buf
        n_keys = live_keys(diag_off)
        k = k_ref[pl.ds(pl.multiple_of(j * tk, tk), n_keys), :]
        s_t = lax.dot_general(k, halves[h][0][c0:c0 + cq], NT_DIMS, preferred_element_type=F32)
        if diag_off is not None:
            key = lax.broadcasted_iota(jnp.int32, s_t.shape, 0)
            query = lax.broadcasted_iota(jnp.int32, s_t.shape, 1) + diag_off
            s_t = jnp.where(key <= query, s_t, -jnp.inf)
        s_ref[h, :n_keys, c0:c0 + cq] = s_t
        x_ref[h, :, c0:c0 + cq] = jnp.max(s_t, axis=0, keepdims=True)

    def consume(buf, j, c0, diag_off, h):
        s_ref, x_ref = buf
        _, m_ref, a_ref = halves[h]
        n_keys = live_keys(diag_off)
        m_old = m_ref[:, c0:c0 + cq]
        m_new = jnp.maximum(m_old, x_ref[h, :, c0:c0 + cq])
        alpha = jnp.exp2(m_old - m_new)
        p_t = jnp.exp2(s_ref[h, :n_keys, c0:c0 + cq] - m_new).astype(BF16)
        a_ref[:, c0:c0 + cq] = alpha * a_ref[:, c0:c0 + cq] + jnp.dot(
            vt_ref[j, :, :n_keys], p_t, preferred_element_type=F32)
        m_ref[:, c0:c0 + cq] = m_new

    def chunks(diag):
        if diag is None:
            return [(c0, None) for c0 in range(0, tq, cq)]
        first = diag * tk
        return [(c0, c0 - first if c0 < first + tk else None) for c0 in range(first, tq, cq)]

    def overlap(produced, consumed):
        puts = [] if produced is None else [
            (produced[0], produced[1], c0, off) for c0, off in chunks(produced[2])]
        gets = [] if consumed is None else [
            (consumed[0], consumed[1], c0, off) for c0, off in chunks(consumed[2])]
        for n in range(max(len(puts), len(gets))):
            for h in range(len(halves)):
                if n < len(puts):
                    produce(*puts[n], h)
                if n < len(gets):
                    consume(*gets[n], h)

    buf_a, buf_b = (sa_ref, xa_ref), (sb_ref, xb_ref)
    n_full = i * diag_blocks

    @pl.when(i == 0)
    def _():
        overlap((buf_a, 0, 0), None)

    @pl.when(i > 0)
    def _():
        overlap((buf_a, 0, None), None)

    def pair(t, carry):
        j = 2 * t
        overlap((buf_b, j + 1, None), (buf_a, j, None))
        overlap((buf_a, j + 2, None), (buf_b, j + 1, None))
        return carry

    lax.fori_loop(0, n_full // 2 - 1, pair, 0)

    @pl.when(i > 0)
    def _():
        j = n_full - 2
        overlap((buf_b, j + 1, None), (buf_a, j, None))
        overlap((buf_a, j + 2, 0), (buf_b, j + 1, None))

    bufs = (buf_a, buf_b)
    for d in range(diag_blocks):
        following = (bufs[(d + 1) % 2], n_full + d + 1, d + 1) if d + 1 < diag_blocks else None
        overlap(following, (bufs[d % 2], n_full + d, d))

    lq = lam_ref[...]
    lam = (jnp.exp(jnp.sum(lq[0:1] * lq[1:2], axis=1, keepdims=True))
           - jnp.exp(jnp.sum(lq[2:3] * lq[3:4], axis=1, keepdims=True)) + lambda_init)
    a0 = a0_ref[...]
    a1 = a1_ref[...]
    o_t = (a0[:DA_VDIM] / a0[DA_VDIM:DA_VDIM + 1]
           - lam * (a1[:DA_VDIM] / a1[DA_VDIM:DA_VDIM + 1]))
    o_t = o_t * lax.rsqrt(jnp.mean(o_t * o_t, axis=0, keepdims=True) + SUBLN_EPS)
    o_t = o_t * g_ref[...] * (1.0 - lambda_init)
    o_ref[...] = o_t.T.astype(BF16)


def _diff_attention(qa, ka, va, lambda_qk, subln_col, layer, lambda_init, tq, tk, cq):
    b, s, width = qa.shape
    heads = width // DA_VDIM
    assert DA_VDIM == LANES and s % tq == 0 and tq % (2 * tk) == 0 and tk % cq == 0
    kv_spec = pl.BlockSpec((None, s, DA_VDIM), lambda bi, h, i: (bi, 0, h))
    stat = pltpu.VMEM((1, tq), F32)
    acc = pltpu.VMEM((DA_VDIM + BF16_ROWS, tq), F32)
    scores = pltpu.VMEM((2, tk, tq), F32)
    block_max = pltpu.VMEM((2, 1, tq), F32)
    return pl.pallas_call(
        functools.partial(_diffattn_kernel, lambda_init=lambda_init, tq=tq, tk=tk, cq=cq),
        grid=(b, heads, s // tq),
        in_specs=[pl.BlockSpec((None, tq, DA_VDIM), lambda bi, h, i: (bi, i, h)),
                  kv_spec, kv_spec,
                  pl.BlockSpec((None, 4, HEAD_DIM), lambda bi, h, i: (layer, 0, 0)),
                  pl.BlockSpec((None, DA_VDIM, 1), lambda bi, h, i: (layer, 0, 0))],
        out_specs=pl.BlockSpec((None, tq, DA_VDIM), lambda bi, h, i: (bi, i, h)),
        out_shape=jax.ShapeDtypeStruct((b, s, width), BF16),
        scratch_shapes=[pltpu.VMEM((s // tk, DA_VDIM + BF16_ROWS, tk), BF16),
                        stat, acc, stat, acc, scores, block_max, scores, block_max],
        compiler_params=_params(3),
        name="diff_attention",
    )(qa, ka, va, lambda_qk, subln_col)


def _swa_kernel(q_ref, kc_ref, vc_ref, kp_ref, vp_ref, sink_ref, o_ref, *, tq):
    i = pl.program_id(1)
    kcat = jnp.concatenate([kp_ref[...], kc_ref[...]], axis=0)
    vcat = jnp.concatenate([vp_ref[...], vc_ref[...]], axis=0)
    n_r = tq // WINDOW
    rows_per = SW_GROUP * WINDOW
    stacked = (n_r * rows_per, 2 * WINDOW)
    row = lax.broadcasted_iota(jnp.int32, stacked, 0)
    qi = row % WINDOW
    kj = lax.broadcasted_iota(jnp.int32, stacked, 1)
    before_start = (row < rows_per) & (i == 0) & (kj < WINDOW)
    keep = (kj > qi) & (kj <= qi + WINDOW) & jnp.logical_not(before_start)
    bias = jnp.where(keep, 0.0, -jnp.inf).astype(F32)
    sinks = sink_ref[...]
    ones = jnp.ones((2 * WINDOW, LANES), BF16)
    pad = jnp.zeros((2 * WINDOW, LANES - HEAD_DIM), BF16)
    group_w = SW_GROUP * HEAD_DIM
    for kvh in range(SW_KV_HEADS):
        cols = slice(kvh * HEAD_DIM, (kvh + 1) * HEAD_DIM)
        sink_blk = jnp.concatenate(
            [jnp.broadcast_to(sinks[:, kvh * SW_GROUP + g:kvh * SW_GROUP + g + 1], (WINDOW, LANES))
             for g in range(SW_GROUP)], axis=0)
        sink_col = jnp.concatenate([sink_blk] * n_r, axis=0)
        scores = []
        for r in range(n_r):
            k = kcat[r * WINDOW:(r + 2) * WINDOW, cols]
            qblk = q_ref[r * WINDOW:(r + 1) * WINDOW, kvh * group_w:(kvh + 1) * group_w]
            q = jnp.concatenate([qblk[:, g * HEAD_DIM:(g + 1) * HEAD_DIM]
                                 for g in range(SW_GROUP)], axis=0)
            scores.append(lax.dot_general(q, k, NT_DIMS, preferred_element_type=F32))
        s = jnp.concatenate(scores, axis=0) + bias
        m = jnp.maximum(jnp.max(s, axis=1, keepdims=True), sink_col)
        p = jnp.exp(s - jnp.concatenate([m, m], axis=1)).astype(BF16)
        pvs = []
        for r in range(n_r):
            v = vcat[r * WINDOW:(r + 2) * WINDOW, cols]
            v_ones = jnp.concatenate([v, pad, ones], axis=1)
            pvs.append(jnp.dot(p[r * rows_per:(r + 1) * rows_per], v_ones,
                               preferred_element_type=F32))
        pv = jnp.concatenate(pvs, axis=0)
        denom = pv[:, LANES:] + jnp.exp(sink_col - m)
        o = (pv[:, :HEAD_DIM] / denom[:, :HEAD_DIM]).astype(BF16)
        for r in range(n_r):
            o_ref[r * WINDOW:(r + 1) * WINDOW, kvh * group_w:(kvh + 1) * group_w] = (
                jnp.concatenate([o[(r * SW_GROUP + g) * WINDOW:(r * SW_GROUP + g + 1) * WINDOW]
                                 for g in range(SW_GROUP)], axis=1))


def _swa_attention(qs, ksw, vsw, sinks, layer, tq):
    b, s, width = qs.shape
    kv_width = ksw.shape[-1]
    per = tq // WINDOW
    row = lambda bi, i: (bi, i, 0)
    prev = lambda bi, i: (bi, jnp.maximum(i * per - 1, 0), 0)
    return pl.pallas_call(
        functools.partial(_swa_kernel, tq=tq),
        grid=(b, s // tq),
        in_specs=[pl.BlockSpec((None, tq, width), row),
                  pl.BlockSpec((None, tq, kv_width), row),
                  pl.BlockSpec((None, tq, kv_width), row),
                  pl.BlockSpec((None, WINDOW, kv_width), prev),
                  pl.BlockSpec((None, WINDOW, kv_width), prev),
                  pl.BlockSpec((None, 1, SW_Q_HEADS), lambda bi, i: (layer, 0, 0))],
        out_specs=pl.BlockSpec((None, tq, width), row),
        out_shape=jax.ShapeDtypeStruct((b, s, width), BF16),
        compiler_params=_params(2),
        name="swa_attention",
    )(qs, ksw, vsw, ksw, vsw, sinks)


def _merge_kernel(x_ref, oa_ref, ob_ref, gate_ref, mod_ref, wa_ref, wb_ref, wo_ref, o_ref):
    d = x_ref.shape[-1]
    a = jnp.dot(oa_ref[...], wa_ref[...], preferred_element_type=F32)
    bb = jnp.dot(ob_ref[...], wb_ref[...], preferred_element_type=F32)
    mixed = gate_ref[:, :d].astype(F32) * a + gate_ref[:, d:].astype(F32) * bb
    y = jnp.dot(mixed.astype(BF16), wo_ref[...], preferred_element_type=F32)
    o_ref[...] = x_ref[...] + mod_ref[2:3, :] * y


def _merge_out(x, oa, ob, gates, mod_l, wa_bf, wb_bf, wo_bf, layer, tm):
    b, s, d = x.shape
    row = lambda bi, i: (bi, i, 0)
    wspec = lambda w: _resident((None,) + w.shape[1:], lambda bi, i: (layer, 0, 0))
    return pl.pallas_call(
        _merge_kernel,
        grid=(b, s // tm),
        in_specs=[pl.BlockSpec((None, tm, d), row),
                  pl.BlockSpec((None, tm, oa.shape[-1]), row),
                  pl.BlockSpec((None, tm, ob.shape[-1]), row),
                  pl.BlockSpec((None, tm, 2 * d), row),
                  pl.BlockSpec((None, N_MOD, d), lambda bi, i: (bi, 0, 0)),
                  wspec(wa_bf), wspec(wb_bf), wspec(wo_bf)],
        out_specs=pl.BlockSpec((None, tm, d), row),
        out_shape=jax.ShapeDtypeStruct((b, s, d), F32),
        compiler_params=_params(2),
        name="merge_out",
    )(x, oa, ob, gates, mod_l, wa_bf, wb_bf, wo_bf)


def _ffn_kernel(x_ref, halo_ref, mod_ref, g_ref, wg_ref, wu_ref, cw_ref, cb_ref, wd_ref, o_ref,
                h_ref, act_ref, *, chunk):
    i = pl.program_id(1)
    mod = mod_ref[...]
    shift, scale, gate = mod[3:4], mod[4:5], mod[5:6]
    gain = g_ref[...]

    def normed(rows):
        return _rms(rows, NORM_EPS) * gain * (1.0 + scale) + shift

    x = x_ref[...]
    halo = jnp.where(i > 0, normed(halo_ref[...]), 0.0)
    h_ref[0:BF16_ROWS, :] = halo.astype(BF16)
    h_ref[BF16_ROWS:, :] = normed(x).astype(BF16)

    d_ff = wg_ref.shape[-1]
    for c0 in range(0, d_ff, chunk):
        cols = slice(c0, c0 + chunk)
        g = jnp.dot(h_ref[...], wg_ref[:, cols], preferred_element_type=F32)
        u = jnp.dot(h_ref[BF16_ROWS:, :], wu_ref[:, cols], preferred_element_type=F32)
        gc = cb_ref[:, cols] + cw_ref[CONV_W - 1:CONV_W, cols] * g[BF16_ROWS:]
        for t in range(CONV_W - 1):
            back = CONV_W - 1 - t
            gc = gc + cw_ref[t:t + 1, cols] * pltpu.roll(g, back, axis=0)[BF16_ROWS:]
        act_ref[:, cols] = ((gc * jax.nn.sigmoid(gc)) * u).astype(BF16)

    y = jnp.dot(act_ref[...], wd_ref[...], preferred_element_type=F32)
    o_ref[...] = x + gate * y


def _conv_ffn(x, mod_l, g, wg_bf, wu_bf, conv_w, conv_b, wd_bf, layer, tm):
    b, s, d = x.shape
    d_ff = wg_bf.shape[-1]
    per = tm // BF16_ROWS
    row = lambda bi, i: (bi, i, 0)
    wspec = lambda w: _resident((None,) + w.shape[1:], lambda bi, i: (layer, 0, 0))
    return pl.pallas_call(
        functools.partial(_ffn_kernel, chunk=_ffn_chunk_width(d_ff)),
        grid=(b, s // tm),
        in_specs=[pl.BlockSpec((None, tm, d), row),
                  pl.BlockSpec((None, BF16_ROWS, d),
                               lambda bi, i: (bi, jnp.maximum(i * per - 1, 0), 0)),
                  pl.BlockSpec((None, N_MOD, d), lambda bi, i: (bi, 0, 0)),
                  pl.BlockSpec((None, 1, d), lambda bi, i: (layer, 0, 0)),
                  wspec(wg_bf), wspec(wu_bf), wspec(conv_w), wspec(conv_b), wspec(wd_bf)],
        out_specs=pl.BlockSpec((None, tm, d), row),
        out_shape=jax.ShapeDtypeStruct((b, s, d), F32),
        scratch_shapes=[pltpu.VMEM((BF16_ROWS + tm, d), BF16), pltpu.VMEM((tm, d_ff), BF16)],
        compiler_params=_params(2),
        name="conv_ffn",
    )(x, x, mod_l, g, wg_bf, wu_bf, conv_w, conv_b, wd_bf)


def _final_norm_kernel(x_ref, g_ref, o_ref):
    o_ref[...] = _rms(x_ref[...], NORM_EPS) * g_ref[...]


def _final_norm(x, g, tm):
    b, s, d = x.shape
    row = lambda bi, i: (bi, i, 0)
    return pl.pallas_call(
        _final_norm_kernel,
        grid=(b, s // tm),
        in_specs=[pl.BlockSpec((None, tm, d), row), pl.BlockSpec((1, d), lambda bi, i: (0, 0))],
        out_specs=pl.BlockSpec((None, tm, d), row),
        out_shape=jax.ShapeDtypeStruct((b, s, d), F32),
        compiler_params=_params(2),
        name="final_norm",
    )(x, g.reshape(1, d))


def _ffn_chunk_width(d_ff):
    for cw in (512, 384, 256, 128):
        if d_ff % cw == 0:
            return cw
    raise ValueError(f"d_ff={d_ff} is not a multiple of {LANES}")


def kernel(x, c, positions, w_ada, b_ada, attn_norm, w_in, lambda_qk, subln_norm, sinks,
           w_branch_a, w_branch_b, w_out, ffn_norm, w_gate, w_up, conv_w, conv_b, w_down,
           final_norm):
    b, s, d = x.shape
    depth = w_in.shape[0]
    d_ff = w_gate.shape[-1]
    tm = min(s, 512)
    tq_da, tk_da, cq_da = min(s, 2048), 512, 256
    tq_sw = min(s, 512)

    mod = _adaln_mod(c, w_ada, b_ada)
    cos, sin_signed = _rope_tables(positions)

    w_in_bf = w_in.astype(BF16)
    wa_bf, wb_bf, wo_bf = (w.astype(BF16) for w in (w_branch_a, w_branch_b, w_out))
    wg_bf, wu_bf, wd_bf = (w.astype(BF16) for w in (w_gate, w_up, w_down))
    conv_b3 = conv_b.reshape(depth, 1, d_ff)
    attn_g = attn_norm.reshape(depth, 1, d)
    ffn_g = ffn_norm.reshape(depth, 1, d)
    subln_col = subln_norm.reshape(depth, DA_VDIM, 1)
    sinks3 = sinks.reshape(depth, 1, SW_Q_HEADS)

    for l in range(depth):
        lambda_init = 0.8 - 0.6 * math.exp(-0.3 * l)
        mod_l = mod[l]
        qa, ka, va, qs, ksw, vsw, gates = _in_projection(x, mod_l, attn_g, cos, sin_signed,
                                                        w_in_bf, l, tm)
        oa = _diff_attention(qa, ka, va, lambda_qk, subln_col, l, lambda_init, tq_da, tk_da,
                             cq_da)
        ob = _swa_attention(qs, ksw, vsw, sinks3, l, tq_sw)
        x = _merge_out(x, oa, ob, gates, mod_l, wa_bf, wb_bf, wo_bf, l, tm)
        x = _conv_ffn(x, mod_l, ffn_g, wg_bf, wu_bf, conv_w, conv_b3, wd_bf, l, tm)
    return _final_norm(x, final_norm, tm)
```

```python
import functools
import math

import jax
import jax.numpy as jnp
from jax import lax
from jax.experimental import pallas as pl
from jax.experimental.pallas import tpu as pltpu

F32 = jnp.float32
BF16 = jnp.bfloat16

HEAD_DIM = 64
DA_HEADS = 8
DA_VDIM = 2 * HEAD_DIM
SW_Q_HEADS = 16
SW_KV_HEADS = 4
SW_GROUP = SW_Q_HEADS // SW_KV_HEADS
WINDOW = 128
CONV_W = 3
ROPE_THETA = 10000.0
NORM_EPS = 1e-6
SUBLN_EPS = 1e-5
N_MOD = 6
QK_SCALE = HEAD_DIM ** -0.5
LOG2_E = math.log2(math.e)

LANES = 128
BF16_ROWS = 16
VMEM_LIMIT = 56 * 1024 * 1024

NT_DIMS = (((1,), (1,)), ((), ()))


def _params(n_grid):
    return pltpu.CompilerParams(dimension_semantics=("arbitrary",) * n_grid,
                                vmem_limit_bytes=VMEM_LIMIT)


def _resident(shape, index_map):
    return pl.BlockSpec(shape, index_map, pipeline_mode=pl.Buffered(1))


def _rms(x, eps):
    return x * lax.rsqrt(jnp.mean(x * x, axis=-1, keepdims=True) + eps)


def _mod_kernel(c_ref, w_ref, b_ref, o_ref):
    c = c_ref[...]
    ca = c * jax.nn.sigmoid(c)
    o_ref[...] = jnp.dot(ca.astype(BF16), w_ref[...].astype(BF16),
                         preferred_element_type=F32) + b_ref[...]


def _adaln_mod(c, w_ada, b_ada):
    depth, d, width = w_ada.shape
    b = c.shape[0]
    rows = -(-b // 8) * 8
    c_pad = jnp.zeros((rows, d), F32).at[:b].set(c)
    tn = 1536
    out = pl.pallas_call(
        _mod_kernel,
        grid=(depth, width // tn),
        in_specs=[pl.BlockSpec((rows, d), lambda l, n: (0, 0)),
                  pl.BlockSpec((None, d, tn), lambda l, n: (l, 0, n)),
                  pl.BlockSpec((None, 1, tn), lambda l, n: (l, 0, n))],
        out_specs=pl.BlockSpec((None, rows, tn), lambda l, n: (l, 0, n)),
        out_shape=jax.ShapeDtypeStruct((depth, rows, width), F32),
        compiler_params=_params(2),
        name="adaln_mod",
    )(c_pad, w_ada, b_ada.reshape(depth, 1, width))
    return out[:, :b].reshape(depth, b, N_MOD, d)


def _rope_kernel(pos_ref, invf_ref, sign_ref, cos_ref, sin_ref):
    ang = pos_ref[...].astype(F32) * invf_ref[...]
    cos_ref[...] = jnp.cos(ang)
    sin_ref[...] = jnp.sin(ang) * sign_ref[...]


def _rope_tables(positions):
    b, s = positions.shape
    inv_freq = ROPE_THETA ** (-jnp.arange(0, HEAD_DIM, 2, dtype=F32) / HEAD_DIM)
    reps = LANES // (HEAD_DIM // 2)
    invf = jnp.tile(inv_freq, reps).reshape(1, LANES)
    half = HEAD_DIM // 2
    sign = jnp.tile(jnp.concatenate([-jnp.ones((half,), F32), jnp.ones((half,), F32)]),
                    LANES // HEAD_DIM).reshape(1, LANES)
    tm = min(s, 1024)
    return pl.pallas_call(
        _rope_kernel,
        grid=(b, s // tm),
        in_specs=[pl.BlockSpec((None, tm, 1), lambda bi, i: (bi, i, 0)),
                  pl.BlockSpec((1, LANES), lambda bi, i: (0, 0)),
                  pl.BlockSpec((1, LANES), lambda bi, i: (0, 0))],
        out_specs=[pl.BlockSpec((None, tm, LANES), lambda bi, i: (bi, i, 0))] * 2,
        out_shape=[jax.ShapeDtypeStruct((b, s, LANES), F32)] * 2,
        compiler_params=_params(2),
        name="rope_tables",
    )(positions.reshape(b, s, 1), invf, sign)


def _rope_block(a, cos, sin_signed, first_half):
    swapped = jnp.where(first_half, pltpu.roll(a, LANES - HEAD_DIM // 2, axis=1),
                        pltpu.roll(a, HEAD_DIM // 2, axis=1))
    return a * cos + swapped * sin_signed


def _inproj_kernel(x_ref, mod_ref, g_ref, cos_ref, sin_ref, w_ref,
                   qa_ref, ka_ref, va_ref, qs_ref, ks_ref, vs_ref, gate_ref, h_ref,
                   *, sections, chunk):
    x = x_ref[...]
    mod = mod_ref[...]
    shift, scale = mod[0:1], mod[1:2]
    h = _rms(x, NORM_EPS) * g_ref[...] * (1.0 + scale) + shift
    h_ref[...] = h.astype(BF16)

    cos = cos_ref[...]
    sin_signed = sin_ref[...]
    lane = lax.broadcasted_iota(jnp.int32, cos.shape, 1)
    first_half = (lane % HEAD_DIM) < (HEAD_DIM // 2)

    outs = (qa_ref, ka_ref, va_ref, qs_ref, ks_ref, vs_ref, gate_ref)
    col = 0
    for out_ref, (width, kind, mult) in zip(outs, sections):
        for c0 in range(0, width, chunk):
            cw = min(chunk, width - c0)
            acc = jnp.dot(h_ref[...], w_ref[:, col + c0:col + c0 + cw],
                          preferred_element_type=F32)
            if kind == "rope":
                for j in range(cw // LANES):
                    blk = _rope_block(acc[:, j * LANES:(j + 1) * LANES], cos, sin_signed,
                                      first_half)
                    if mult != 1.0:
                        blk = blk * mult
                    out_ref[:, c0 + j * LANES:c0 + (j + 1) * LANES] = blk.astype(BF16)
            elif kind == "sigmoid":
                out_ref[:, c0:c0 + cw] = jax.nn.sigmoid(acc).astype(BF16)
            else:
                out_ref[:, c0:c0 + cw] = acc.astype(BF16)
        col += width


def _in_projection(x, mod_l, g, cos, sin_signed, w_in_bf, layer, tm):
    b, s, d = x.shape
    da_qk = DA_HEADS * 2 * HEAD_DIM
    da_v = DA_HEADS * DA_VDIM
    sw_q = SW_Q_HEADS * HEAD_DIM
    sw_kv = SW_KV_HEADS * HEAD_DIM
    sections = ((da_qk, "rope", QK_SCALE * LOG2_E), (da_qk, "rope", 1.0), (da_v, "plain", 1.0),
                (sw_q, "rope", QK_SCALE), (sw_kv, "rope", 1.0), (sw_kv, "plain", 1.0),
                (2 * d, "sigmoid", 1.0))
    widths = [w for w, _, _ in sections]
    in_width = sum(widths)
    assert w_in_bf.shape[1:] == (d, in_width)
    row = lambda bi, i: (bi, i, 0)
    return pl.pallas_call(
        functools.partial(_inproj_kernel, sections=sections, chunk=512),
        grid=(b, s // tm),
        in_specs=[pl.BlockSpec((None, tm, d), row),
                  pl.BlockSpec((None, N_MOD, d), lambda bi, i: (bi, 0, 0)),
                  pl.BlockSpec((None, 1, d), lambda bi, i: (layer, 0, 0)),
                  pl.BlockSpec((None, tm, LANES), row),
                  pl.BlockSpec((None, tm, LANES), row),
                  _resident((None, d, in_width), lambda bi, i: (layer, 0, 0))],
        out_specs=[pl.BlockSpec((None, tm, w), row) for w in widths],
        out_shape=[jax.ShapeDtypeStruct((b, s, w), BF16) for w in widths],
        scratch_shapes=[pltpu.VMEM((tm, d), BF16)],
        compiler_params=_params(2),
        name="in_projection",
    )(x, mod_l, g, cos, sin_signed, w_in_bf)


def _diffattn_kernel(q_ref, k_ref, v_ref, lam_ref, g_ref, o_ref,
                     vt_ref, m0_ref, a0_ref, m1_ref, a1_ref, sa_ref, xa_ref, sb_ref, xb_ref,
                     *, lambda_init, tq, tk, cq):
    i = pl.program_id(2)
    diag_blocks = tq // tk

    @pl.when(i == 0)
    def _():
        def transpose_block(j, carry):
            v = v_ref[pl.ds(pl.multiple_of(j * tk, tk), tk), :].astype(F32)
            vt_ref[j, 0:DA_VDIM, :] = v.T.astype(BF16)
            vt_ref[j, DA_VDIM:, :] = jnp.ones((BF16_ROWS, tk), BF16)
            return carry
        lax.fori_loop(0, vt_ref.shape[0], transpose_block, 0)

    q = q_ref[...]
    lane = lax.broadcasted_iota(jnp.int32, q.shape, 1)
    zero = jnp.zeros_like(q)
    halves = ((jnp.where(lane < HEAD_DIM, q, zero), m0_ref, a0_ref),
              (jnp.where(lane >= HEAD_DIM, q, zero), m1_ref, a1_ref))
    for _, m_ref, a_ref in halves:
        m_ref[...] = jnp.full(m_ref.shape, -jnp.inf, F32)
        a_ref[...] = jnp.zeros(a_ref.shape, F32)

    def live_keys(diag_off):
        return tk if diag_off is None else min(tk, diag_off + cq)

    def produce(buf, j, c0, diag_off, h):
        s_ref, x_ref = buf
        n_keys = live_keys(diag_off)
        k = k_ref[pl.ds(pl.multiple_of(j * tk, tk), n_keys), :]
        s_t = lax.dot_general(k, halves[h][0][c0:c0 + cq], NT_DIMS, preferred_element_type=F32)
        if diag_off is not None:
            key = lax.broadcasted_iota(jnp.int32, s_t.shape, 0)
            query = lax.broadcasted_iota(jnp.int32, s_t.shape, 1) + diag_off
            s_t = jnp.where(key <= query, s_t, -jnp.inf)
        s_ref[h, :n_keys, c0:c0 + cq] = s_t
        x_ref[h, :, c0:c0 + cq] = jnp.max(s_t, axis=0, keepdims=True)

    def consume(buf, j, c0, diag_off, h):
        s_ref, x_ref = buf
        _, m_ref, a_ref = halves[h]
        n_keys = live_keys(diag_off)
        m_old = m_ref[:, c0:c0 + cq]
        m_new = jnp.maximum(m_old, x_ref[h, :, c0:c0 + cq])
        alpha = jnp.exp2(m_old - m_new)
        p_t = jnp.exp2(s_ref[h, :n_keys, c0:c0 + cq] - m_new).astype(BF16)
        a_ref[:, c0:c0 + cq] = alpha * a_ref[:, c0:c0 + cq] + jnp.dot(
            vt_ref[j, :, :n_keys], p_t, preferred_element_type=F32)
        m_ref[:, c0:c0 + cq] = m_new

    def chunks(diag):
        if diag is None:
            return [(c0, None) for c0 in range(0, tq, cq)]
        first = diag * tk
        return [(c0, c0 - first if c0 < first + tk else None) for c0 in range(first, tq, cq)]

    def overlap(produced, consumed):
        puts = [] if produced is None else [
            (produced[0], produced[1], c0, off) for c0, off in chunks(produced[2])]
        gets = [] if consumed is None else [
            (consumed[0], consumed[1], c0, off) for c0, off in chunks(consumed[2])]
        for n in range(max(len(puts), len(gets))):
            for h in range(len(halves)):
                if n < len(puts):
                    produce(*puts[n], h)
                if n < len(gets):
                    consume(*gets[n], h)

    buf_a, buf_b = (sa_ref, xa_ref), (sb_ref, xb_ref)
    n_full = i * diag_blocks

    @pl.when(i == 0)
    def _():
        overlap((buf_a, 0, 0), None)

    @pl.when(i > 0)
    def _():
        overlap((buf_a, 0, None), None)

    def pair(t, carry):
        j = 2 * t
        overlap((buf_b, j + 1, None), (buf_a, j, None))
        overlap((buf_a, j + 2, None), (buf_b, j + 1, None))
        return carry

    lax.fori_loop(0, n_full // 2 - 1, pair, 0)

    @pl.when(i > 0)
    def _():
        j = n_full - 2
        overlap((buf_b, j + 1, None), (buf_a, j, None))
        overlap((buf_a, j + 2, 0), (buf_b, j + 1, None))

    bufs = (buf_a, buf_b)
    for d in range(diag_blocks):
        following = (bufs[(d + 1) % 2], n_full + d + 1, d + 1) if d + 1 < diag_blocks else None
        overlap(following, (bufs[d % 2], n_full + d, d))

    lq = lam_ref[...]
    lam = (jnp.exp(jnp.sum(lq[0:1] * lq[1:2], axis=1, keepdims=True))
           - jnp.exp(jnp.sum(lq[2:3] * lq[3:4], axis=1, keepdims=True)) + lambda_init)
    a0 = a0_ref[...]
    a1 = a1_ref[...]
    o_t = (a0[:DA_VDIM] / a0[DA_VDIM:DA_VDIM + 1]
           - lam * (a1[:DA_VDIM] / a1[DA_VDIM:DA_VDIM + 1]))
    o_t = o_t * lax.rsqrt(jnp.mean(o_t * o_t, axis=0, keepdims=True) + SUBLN_EPS)
    o_t = o_t * g_ref[...] * (1.0 - lambda_init)
    o_ref[...] = o_t.T.astype(BF16)


def _diff_attention(qa, ka, va, lambda_qk, subln_col, layer, lambda_init, tq, tk, cq):
    b, s, width = qa.shape
    heads = width // DA_VDIM
    assert DA_VDIM == LANES and s % tq == 0 and tq % (2 * tk) == 0 and tk % cq == 0
    kv_spec = pl.BlockSpec((None, s, DA_VDIM), lambda bi, h, i: (bi, 0, h))
    stat = pltpu.VMEM((1, tq), F32)
    acc = pltpu.VMEM((DA_VDIM + BF16_ROWS, tq), F32)
    scores = pltpu.VMEM((2, tk, tq), F32)
    block_max = pltpu.VMEM((2, 1, tq), F32)
    return pl.pallas_call(
        functools.partial(_diffattn_kernel, lambda_init=lambda_init, tq=tq, tk=tk, cq=cq),
        grid=(b, heads, s // tq),
        in_specs=[pl.BlockSpec((None, tq, DA_VDIM), lambda bi, h, i: (bi, i, h)),
                  kv_spec, kv_spec,
                  pl.BlockSpec((None, 4, HEAD_DIM), lambda bi, h, i: (layer, 0, 0)),
                  pl.BlockSpec((None, DA_VDIM, 1), lambda bi, h, i: (layer, 0, 0))],
        out_specs=pl.BlockSpec((None, tq, DA_VDIM), lambda bi, h, i: (bi, i, h)),
        out_shape=jax.ShapeDtypeStruct((b, s, width), BF16),
        scratch_shapes=[pltpu.VMEM((s // tk, DA_VDIM + BF16_ROWS, tk), BF16),
                        stat, acc, stat, acc, scores, block_max, scores, block_max],
        compiler_params=_params(3),
        name="diff_attention",
    )(qa, ka, va, lambda_qk, subln_col)


def _swa_kernel(q_ref, kc_ref, vc_ref, kp_ref, vp_ref, sink_ref, o_ref, *, tq):
    i = pl.program_id(1)
    kcat = jnp.concatenate([kp_ref[...], kc_ref[...]], axis=0)
    vcat = jnp.concatenate([vp_ref[...], vc_ref[...]], axis=0)
    n_r = tq // WINDOW
    rows_per = SW_GROUP * WINDOW
    stacked = (n_r * rows_per, 2 * WINDOW)
    row = lax.broadcasted_iota(jnp.int32, stacked, 0)
    qi = row % WINDOW
    kj = lax.broadcasted_iota(jnp.int32, stacked, 1)
    before_start = (row < rows_per) & (i == 0) & (kj < WINDOW)
    keep = (kj > qi) & (kj <= qi + WINDOW) & jnp.logical_not(before_start)
    bias = jnp.where(keep, 0.0, -jnp.inf).astype(F32)
    sinks = sink_ref[...]
    ones = jnp.ones((2 * WINDOW, LANES), BF16)
    pad = jnp.zeros((2 * WINDOW, LANES - HEAD_DIM), BF16)
    group_w = SW_GROUP * HEAD_DIM
    for kvh in range(SW_KV_HEADS):
        cols = slice(kvh * HEAD_DIM, (kvh + 1) * HEAD_DIM)
        sink_blk = jnp.concatenate(
            [jnp.broadcast_to(sinks[:, kvh * SW_GROUP + g:kvh * SW_GROUP + g + 1], (WINDOW, LANES))
             for g in range(SW_GROUP)], axis=0)
        sink_col = jnp.concatenate([sink_blk] * n_r, axis=0)
        scores = []
        for r in range(n_r):
            k = kcat[r * WINDOW:(r + 2) * WINDOW, cols]
            qblk = q_ref[r * WINDOW:(r + 1) * WINDOW, kvh * group_w:(kvh + 1) * group_w]
            q = jnp.concatenate([qblk[:, g * HEAD_DIM:(g + 1) * HEAD_DIM]
                                 for g in range(SW_GROUP)], axis=0)
            scores.append(lax.dot_general(q, k, NT_DIMS, preferred_element_type=F32))
        s = jnp.concatenate(scores, axis=0) + bias
        m = jnp.maximum(jnp.max(s, axis=1, keepdims=True), sink_col)
        p = jnp.exp(s - jnp.concatenate([m, m], axis=1)).astype(BF16)
        pvs = []
        for r in range(n_r):
            v = vcat[r * WINDOW:(r + 2) * WINDOW, cols]
            v_ones = jnp.concatenate([v, pad, ones], axis=1)
            pvs.append(jnp.dot(p[r * rows_per:(r + 1) * rows_per], v_ones,
                               preferred_element_type=F32))
        pv = jnp.concatenate(pvs, axis=0)
        denom = pv[:, LANES:] + jnp.exp(sink_col - m)
        o = (pv[:, :HEAD_DIM] / denom[:, :HEAD_DIM]).astype(BF16)
        for r in range(n_r):
            o_ref[r * WINDOW:(r + 1) * WINDOW, kvh * group_w:(kvh + 1) * group_w] = (
                jnp.concatenate([o[(r * SW_GROUP + g) * WINDOW:(r * SW_GROUP + g + 1) * WINDOW]
                                 for g in range(SW_GROUP)], axis=1))


def _swa_attention(qs, ksw, vsw, sinks, layer, tq):
    b, s, width = qs.shape
    kv_width = ksw.shape[-1]
    per = tq // WINDOW
    row = lambda bi, i: (bi, i, 0)
    prev = lambda bi, i: (bi, jnp.maximum(i * per - 1, 0), 0)
    return pl.pallas_call(
        functools.partial(_swa_kernel, tq=tq),
        grid=(b, s // tq),
        in_specs=[pl.BlockSpec((None, tq, width), row),
                  pl.BlockSpec((None, tq, kv_width), row),
                  pl.BlockSpec((None, tq, kv_width), row),
                  pl.BlockSpec((None, WINDOW, kv_width), prev),
                  pl.BlockSpec((None, WINDOW, kv_width), prev),
                  pl.BlockSpec((None, 1, SW_Q_HEADS), lambda bi, i: (layer, 0, 0))],
        out_specs=pl.BlockSpec((None, tq, width), row),
        out_shape=jax.ShapeDtypeStruct((b, s, width), BF16),
        compiler_params=_params(2),
        name="swa_attention",
    )(qs, ksw, vsw, ksw, vsw, sinks)


def _merge_kernel(x_ref, oa_ref, ob_ref, gate_ref, mod_ref, wa_ref, wb_ref, wo_ref, o_ref):
    d = x_ref.shape[-1]
    a = jnp.dot(oa_ref[...], wa_ref[...], preferred_element_type=F32)
    bb = jnp.dot(ob_ref[...], wb_ref[...], preferred_element_type=F32)
    mixed = gate_ref[:, :d].astype(F32) * a + gate_ref[:, d:].astype(F32) * bb
    y = jnp.dot(mixed.astype(BF16), wo_ref[...], preferred_element_type=F32)
    o_ref[...] = x_ref[...] + mod_ref[2:3, :] * y


def _merge_out(x, oa, ob, gates, mod_l, wa_bf, wb_bf, wo_bf, layer, tm):
    b, s, d = x.shape
    row = lambda bi, i: (bi, i, 0)
    wspec = lambda w: _resident((None,) + w.shape[1:], lambda bi, i: (layer, 0, 0))
    return pl.pallas_call(
        _merge_kernel,
        grid=(b, s // tm),
        in_specs=[pl.BlockSpec((None, tm, d), row),
                  pl.BlockSpec((None, tm, oa.shape[-1]), row),
                  pl.BlockSpec((None, tm, ob.shape[-1]), row),
                  pl.BlockSpec((None, tm, 2 * d), row),
                  pl.BlockSpec((None, N_MOD, d), lambda bi, i: (bi, 0, 0)),
                  wspec(wa_bf), wspec(wb_bf), wspec(wo_bf)],
        out_specs=pl.BlockSpec((None, tm, d), row),
        out_shape=jax.ShapeDtypeStruct((b, s, d), F32),
        compiler_params=_params(2),
        name="merge_out",
    )(x, oa, ob, gates, mod_l, wa_bf, wb_bf, wo_bf)


def _ffn_kernel(x_ref, halo_ref, mod_ref, g_ref, wg_ref, wu_ref, cw_ref, cb_ref, wd_ref, fin_ref,
                o_ref, h_ref, act_ref, *, chunk, last_layer):
    i = pl.program_id(1)
    mod = mod_ref[...]
    shift, scale, gate = mod[3:4], mod[4:5], mod[5:6]
    gain = g_ref[...]

    def normed(rows):
        return _rms(rows, NORM_EPS) * gain * (1.0 + scale) + shift

    x = x_ref[...]
    halo = jnp.where(i > 0, normed(halo_ref[...]), 0.0)
    h_ref[0:BF16_ROWS, :] = halo.astype(BF16)
    h_ref[BF16_ROWS:, :] = normed(x).astype(BF16)

    d_ff = wg_ref.shape[-1]
    for c0 in range(0, d_ff, chunk):
        cols = slice(c0, c0 + chunk)
        g = jnp.dot(h_ref[...], wg_ref[:, cols], preferred_element_type=F32)
        u = jnp.dot(h_ref[BF16_ROWS:, :], wu_ref[:, cols], preferred_element_type=F32)
        gc = cb_ref[:, cols] + cw_ref[CONV_W - 1:CONV_W, cols] * g[BF16_ROWS:]
        for t in range(CONV_W - 1):
            back = CONV_W - 1 - t
            gc = gc + cw_ref[t:t + 1, cols] * pltpu.roll(g, back, axis=0)[BF16_ROWS:]
        act_ref[:, cols] = ((gc * jax.nn.sigmoid(gc)) * u).astype(BF16)

    y = jnp.dot(act_ref[...], wd_ref[...], preferred_element_type=F32)
    out = x + gate * y
    if last_layer:
        out = _rms(out, NORM_EPS) * fin_ref[...]
    o_ref[...] = out


def _conv_ffn(x, mod_l, g, wg_bf, wu_bf, conv_w, conv_b, wd_bf, final_g, layer, tm, last_layer):
    b, s, d = x.shape
    d_ff = wg_bf.shape[-1]
    per = tm // BF16_ROWS
    row = lambda bi, i: (bi, i, 0)
    wspec = lambda w: _resident((None,) + w.shape[1:], lambda bi, i: (layer, 0, 0))
    return pl.pallas_call(
        functools.partial(_ffn_kernel, chunk=_ffn_chunk_width(d_ff), last_layer=last_layer),
        grid=(b, s // tm),
        in_specs=[pl.BlockSpec((None, tm, d), row),
                  pl.BlockSpec((None, BF16_ROWS, d),
                               lambda bi, i: (bi, jnp.maximum(i * per - 1, 0), 0)),
                  pl.BlockSpec((None, N_MOD, d), lambda bi, i: (bi, 0, 0)),
                  pl.BlockSpec((None, 1, d), lambda bi, i: (layer, 0, 0)),
                  wspec(wg_bf), wspec(wu_bf), wspec(conv_w), wspec(conv_b), wspec(wd_bf),
                  pl.BlockSpec((1, d), lambda bi, i: (0, 0))],
        out_specs=pl.BlockSpec((None, tm, d), row),
        out_shape=jax.ShapeDtypeStruct((b, s, d), F32),
        scratch_shapes=[pltpu.VMEM((BF16_ROWS + tm, d), BF16), pltpu.VMEM((tm, d_ff), BF16)],
        compiler_params=_params(2),
        name="conv_ffn",
    )(x, x, mod_l, g, wg_bf, wu_bf, conv_w, conv_b, wd_bf, final_g)


def _ffn_chunk_width(d_ff):
    for cw in (512, 384, 256, 128):
        if d_ff % cw == 0:
            return cw
    raise ValueError(f"d_ff={d_ff} is not a multiple of {LANES}")


def kernel(x, c, positions, w_ada, b_ada, attn_norm, w_in, lambda_qk, subln_norm, sinks,
           w_branch_a, w_branch_b, w_out, ffn_norm, w_gate, w_up, conv_w, conv_b, w_down,
           final_norm):
    b, s, d = x.shape
    depth = w_in.shape[0]
    d_ff = w_gate.shape[-1]
    tm = min(s, 512)
    tm_ffn = min(s, 1024)
    tq_da, tk_da, cq_da = min(s, 2048), 512, 256
    tq_sw = min(s, 512)

    mod = _adaln_mod(c, w_ada, b_ada)
    cos, sin_signed = _rope_tables(positions)

    w_in_bf = w_in.astype(BF16)
    wa_bf, wb_bf, wo_bf = (w.astype(BF16) for w in (w_branch_a, w_branch_b, w_out))
    wg_bf, wu_bf, wd_bf = (w.astype(BF16) for w in (w_gate, w_up, w_down))
    conv_b3 = conv_b.reshape(depth, 1, d_ff)
    attn_g = attn_norm.reshape(depth, 1, d)
    ffn_g = ffn_norm.reshape(depth, 1, d)
    subln_col = subln_norm.reshape(depth, DA_VDIM, 1)
    sinks3 = sinks.reshape(depth, 1, SW_Q_HEADS)
    final_g = final_norm.reshape(1, d)

    for l in range(depth):
        lambda_init = 0.8 - 0.6 * math.exp(-0.3 * l)
        mod_l = mod[l]
        qa, ka, va, qs, ksw, vsw, gates = _in_projection(x, mod_l, attn_g, cos, sin_signed,
                                                        w_in_bf, l, tm)
        oa = _diff_attention(qa, ka, va, lambda_qk, subln_col, l, lambda_init, tq_da, tk_da,
                             cq_da)
        ob = _swa_attention(qs, ksw, vsw, sinks3, l, tq_sw)
        x = _merge_out(x, oa, ob, gates, mod_l, wa_bf, wb_bf, wo_bf, l, tm_ffn)
        x = _conv_ffn(x, mod_l, ffn_g, wg_bf, wu_bf, conv_w, conv_b3, wd_bf, final_g, l, tm_ffn,
                      last_layer=(l == depth - 1))
    return x
```

```python
import functools
import math

import jax
import jax.numpy as jnp
from jax import lax
from jax.experimental import pallas as pl
from jax.experimental.pallas import tpu as pltpu

F32 = jnp.float32
BF16 = jnp.bfloat16

HEAD_DIM = 64
DA_HEADS = 8
DA_VDIM = 2 * HEAD_DIM
SW_Q_HEADS = 16
SW_KV_HEADS = 4
SW_GROUP = SW_Q_HEADS // SW_KV_HEADS
WINDOW = 128
CONV_W = 3
ROPE_THETA = 10000.0
NORM_EPS = 1e-6
SUBLN_EPS = 1e-5
N_MOD = 6
QK_SCALE = HEAD_DIM ** -0.5
LOG2_E = math.log2(math.e)

LANES = 128
BF16_ROWS = 16
VMEM_LIMIT = 56 * 1024 * 1024

NT_DIMS = (((1,), (1,)), ((), ()))


def _params(n_grid):
    return pltpu.CompilerParams(dimension_semantics=("arbitrary",) * n_grid,
                                vmem_limit_bytes=VMEM_LIMIT)


def _resident(shape, index_map):
    return pl.BlockSpec(shape, index_map, pipeline_mode=pl.Buffered(1))


def _rms(x, eps):
    return x * lax.rsqrt(jnp.mean(x * x, axis=-1, keepdims=True) + eps)


def _mod_kernel(c_ref, w_ref, b_ref, o_ref):
    c = c_ref[...]
    ca = c * jax.nn.sigmoid(c)
    o_ref[...] = jnp.dot(ca.astype(BF16), w_ref[...].astype(BF16),
                         preferred_element_type=F32) + b_ref[...]


def _adaln_mod(c, w_ada, b_ada):
    depth, d, width = w_ada.shape
    b = c.shape[0]
    rows = -(-b // 8) * 8
    c_pad = jnp.zeros((rows, d), F32).at[:b].set(c)
    tn = 1536
    out = pl.pallas_call(
        _mod_kernel,
        grid=(depth, width // tn),
        in_specs=[pl.BlockSpec((rows, d), lambda l, n: (0, 0)),
                  pl.BlockSpec((None, d, tn), lambda l, n: (l, 0, n)),
                  pl.BlockSpec((None, 1, tn), lambda l, n: (l, 0, n))],
        out_specs=pl.BlockSpec((None, rows, tn), lambda l, n: (l, 0, n)),
        out_shape=jax.ShapeDtypeStruct((depth, rows, width), F32),
        compiler_params=_params(2),
        name="adaln_mod",
    )(c_pad, w_ada, b_ada.reshape(depth, 1, width))
    return out[:, :b].reshape(depth, b, N_MOD, d)


def _rope_kernel(pos_ref, invf_ref, sign_ref, cos_ref, sin_ref):
    ang = pos_ref[...].astype(F32) * invf_ref[...]
    cos_ref[...] = jnp.cos(ang)
    sin_ref[...] = jnp.sin(ang) * sign_ref[...]


def _rope_tables(positions):
    b, s = positions.shape
    inv_freq = ROPE_THETA ** (-jnp.arange(0, HEAD_DIM, 2, dtype=F32) / HEAD_DIM)
    reps = LANES // (HEAD_DIM // 2)
    invf = jnp.tile(inv_freq, reps).reshape(1, LANES)
    half = HEAD_DIM // 2
    sign = jnp.tile(jnp.concatenate([-jnp.ones((half,), F32), jnp.ones((half,), F32)]),
                    LANES // HEAD_DIM).reshape(1, LANES)
    tm = min(s, 1024)
    return pl.pallas_call(
        _rope_kernel,
        grid=(b, s // tm),
        in_specs=[pl.BlockSpec((None, tm, 1), lambda bi, i: (bi, i, 0)),
                  pl.BlockSpec((1, LANES), lambda bi, i: (0, 0)),
                  pl.BlockSpec((1, LANES), lambda bi, i: (0, 0))],
        out_specs=[pl.BlockSpec((None, tm, LANES), lambda bi, i: (bi, i, 0))] * 2,
        out_shape=[jax.ShapeDtypeStruct((b, s, LANES), F32)] * 2,
        compiler_params=_params(2),
        name="rope_tables",
    )(positions.reshape(b, s, 1), invf, sign)


def _rope_block(a, cos, sin_signed, first_half):
    swapped = jnp.where(first_half, pltpu.roll(a, LANES - HEAD_DIM // 2, axis=1),
                        pltpu.roll(a, HEAD_DIM // 2, axis=1))
    return a * cos + swapped * sin_signed


def _inproj_kernel(x_ref, mod_ref, g_ref, cos_ref, sin_ref, w_ref,
                   qa_ref, ka_ref, va_ref, qs_ref, ks_ref, vs_ref, gate_ref, h_ref,
                   *, sections, chunk):
    x = x_ref[...]
    mod = mod_ref[...]
    shift, scale = mod[0:1], mod[1:2]
    h = _rms(x, NORM_EPS) * g_ref[...] * (1.0 + scale) + shift
    h_ref[...] = h.astype(BF16)

    cos = cos_ref[...]
    sin_signed = sin_ref[...]
    lane = lax.broadcasted_iota(jnp.int32, cos.shape, 1)
    first_half = (lane % HEAD_DIM) < (HEAD_DIM // 2)

    outs = (qa_ref, ka_ref, va_ref, qs_ref, ks_ref, vs_ref, gate_ref)
    col = 0
    for out_ref, (width, kind, mult) in zip(outs, sections):
        for c0 in range(0, width, chunk):
            cw = min(chunk, width - c0)
            acc = jnp.dot(h_ref[...], w_ref[:, col + c0:col + c0 + cw],
                          preferred_element_type=F32)
            if kind == "rope":
                for j in range(cw // LANES):
                    blk = _rope_block(acc[:, j * LANES:(j + 1) * LANES], cos, sin_signed,
                                      first_half)
                    if mult != 1.0:
                        blk = blk * mult
                    out_ref[:, c0 + j * LANES:c0 + (j + 1) * LANES] = blk.astype(BF16)
            elif kind == "sigmoid":
                out_ref[:, c0:c0 + cw] = jax.nn.sigmoid(acc).astype(BF16)
            elif kind == "value_t":
                for j in range(cw // DA_VDIM):
                    head = (c0 + j * DA_VDIM) // DA_VDIM
                    out_ref[head, 0:DA_VDIM, :] = acc[:, j * DA_VDIM:(j + 1) * DA_VDIM].T.astype(BF16)
                    out_ref[head, DA_VDIM:, :] = jnp.ones((BF16_ROWS, acc.shape[0]), BF16)
            else:
                out_ref[:, c0:c0 + cw] = acc.astype(BF16)
        col += width


def _in_projection(x, mod_l, g, cos, sin_signed, w_in_bf, layer, tm):
    b, s, d = x.shape
    da_qk = DA_HEADS * 2 * HEAD_DIM
    da_v = DA_HEADS * DA_VDIM
    sw_q = SW_Q_HEADS * HEAD_DIM
    sw_kv = SW_KV_HEADS * HEAD_DIM
    sections = ((da_qk, "rope", QK_SCALE * LOG2_E), (da_qk, "rope", 1.0), (da_v, "value_t", 1.0),
                (sw_q, "rope", QK_SCALE), (sw_kv, "rope", 1.0), (sw_kv, "plain", 1.0),
                (2 * d, "sigmoid", 1.0))
    widths = [w for w, _, _ in sections]
    in_width = sum(widths)
    assert w_in_bf.shape[1:] == (d, in_width)
    row = lambda bi, i: (bi, i, 0)
    vt_rows = DA_VDIM + BF16_ROWS
    vt_spec = pl.BlockSpec((None, DA_HEADS, None, vt_rows, tm), lambda bi, i: (bi, 0, i, 0, 0))
    vt_shape = jax.ShapeDtypeStruct((b, DA_HEADS, s // tm, vt_rows, tm), BF16)
    return pl.pallas_call(
        functools.partial(_inproj_kernel, sections=sections, chunk=512),
        grid=(b, s // tm),
        in_specs=[pl.BlockSpec((None, tm, d), row),
                  pl.BlockSpec((None, N_MOD, d), lambda bi, i: (bi, 0, 0)),
                  pl.BlockSpec((None, 1, d), lambda bi, i: (layer, 0, 0)),
                  pl.BlockSpec((None, tm, LANES), row),
                  pl.BlockSpec((None, tm, LANES), row),
                  _resident((None, d, in_width), lambda bi, i: (layer, 0, 0))],
        out_specs=[vt_spec if kind == "value_t" else pl.BlockSpec((None, tm, w), row)
                   for w, kind, _ in sections],
        out_shape=[vt_shape if kind == "value_t" else jax.ShapeDtypeStruct((b, s, w), BF16)
                   for w, kind, _ in sections],
        scratch_shapes=[pltpu.VMEM((tm, d), BF16)],
        compiler_params=_params(2),
        name="in_projection",
    )(x, mod_l, g, cos, sin_signed, w_in_bf)


def _diffattn_kernel(q_ref, k_ref, vt_ref, lam_ref, g_ref, o_ref,
                     m0_ref, a0_ref, m1_ref, a1_ref, sa_ref, xa_ref, sb_ref, xb_ref,
                     *, lambda_init, tq, tk, cq):
    i = pl.program_id(2)
    diag_blocks = tq // tk

    q = q_ref[...]
    lane = lax.broadcasted_iota(jnp.int32, q.shape, 1)
    zero = jnp.zeros_like(q)
    halves = ((jnp.where(lane < HEAD_DIM, q, zero), m0_ref, a0_ref),
              (jnp.where(lane >= HEAD_DIM, q, zero), m1_ref, a1_ref))
    for _, m_ref, a_ref in halves:
        m_ref[...] = jnp.full(m_ref.shape, -jnp.inf, F32)
        a_ref[...] = jnp.zeros(a_ref.shape, F32)

    def live_keys(diag_off):
        return tk if diag_off is None else min(tk, diag_off + cq)

    def produce(buf, j, c0, diag_off, h):
        s_ref, x_ref = buf
        n_keys = live_keys(diag_off)
        k = k_ref[pl.ds(pl.multiple_of(j * tk, tk), n_keys), :]
        s_t = lax.dot_general(k, halves[h][0][c0:c0 + cq], NT_DIMS, preferred_element_type=F32)
        if diag_off is not None:
            key = lax.broadcasted_iota(jnp.int32, s_t.shape, 0)
            query = lax.broadcasted_iota(jnp.int32, s_t.shape, 1) + diag_off
            s_t = jnp.where(key <= query, s_t, -jnp.inf)
        s_ref[h, :n_keys, c0:c0 + cq] = s_t
        x_ref[h, :, c0:c0 + cq] = jnp.max(s_t, axis=0, keepdims=True)

    def consume(buf, j, c0, diag_off, h):
        s_ref, x_ref = buf
        _, m_ref, a_ref = halves[h]
        n_keys = live_keys(diag_off)
        m_old = m_ref[:, c0:c0 + cq]
        m_new = jnp.maximum(m_old, x_ref[h, :, c0:c0 + cq])
        alpha = jnp.exp2(m_old - m_new)
        p_t = jnp.exp2(s_ref[h, :n_keys, c0:c0 + cq] - m_new).astype(BF16)
        a_ref[:, c0:c0 + cq] = alpha * a_ref[:, c0:c0 + cq] + jnp.dot(
            vt_ref[j, :, :n_keys], p_t, preferred_element_type=F32)
        m_ref[:, c0:c0 + cq] = m_new

    def chunks(diag):
        if diag is None:
            return [(c0, None) for c0 in range(0, tq, cq)]
        first = diag * tk
        return [(c0, c0 - first if c0 < first + tk else None) for c0 in range(first, tq, cq)]

    def overlap(produced, consumed):
        puts = [] if produced is None else [
            (produced[0], produced[1], c0, off) for c0, off in chunks(produced[2])]
        gets = [] if consumed is None else [
            (consumed[0], consumed[1], c0, off) for c0, off in chunks(consumed[2])]
        for n in range(max(len(puts), len(gets))):
            for h in range(len(halves)):
                if n < len(gets):
                    consume(*gets[n], h)
                if n < len(puts):
                    produce(*puts[n], h)

    buf_a, buf_b = (sa_ref, xa_ref), (sb_ref, xb_ref)
    n_full = i * diag_blocks

    @pl.when(i == 0)
    def _():
        overlap((buf_a, 0, 0), None)

    @pl.when(i > 0)
    def _():
        overlap((buf_a, 0, None), None)

    def pair(t, carry):
        j = 2 * t
        overlap((buf_b, j + 1, None), (buf_a, j, None))
        overlap((buf_a, j + 2, None), (buf_b, j + 1, None))
        return carry

    lax.fori_loop(0, n_full // 2 - 1, pair, 0)

    @pl.when(i > 0)
    def _():
        j = n_full - 2
        overlap((buf_b, j + 1, None), (buf_a, j, None))
        overlap((buf_a, j + 2, 0), (buf_b, j + 1, None))

    bufs = (buf_a, buf_b)
    for d in range(diag_blocks):
        following = (bufs[(d + 1) % 2], n_full + d + 1, d + 1) if d + 1 < diag_blocks else None
        overlap(following, (bufs[d % 2], n_full + d, d))

    lq = lam_ref[...]
    lam = (jnp.exp(jnp.sum(lq[0:1] * lq[1:2], axis=1, keepdims=True))
           - jnp.exp(jnp.sum(lq[2:3] * lq[3:4], axis=1, keepdims=True)) + lambda_init)
    a0 = a0_ref[...]
    a1 = a1_ref[...]
    o_t = (a0[:DA_VDIM] / a0[DA_VDIM:DA_VDIM + 1]
           - lam * (a1[:DA_VDIM] / a1[DA_VDIM:DA_VDIM + 1]))
    o_t = o_t * lax.rsqrt(jnp.mean(o_t * o_t, axis=0, keepdims=True) + SUBLN_EPS)
    o_t = o_t * g_ref[...] * (1.0 - lambda_init)
    o_ref[...] = o_t.T.astype(BF16)


def _diff_attention(qa, ka, va_t, lambda_qk, subln_col, layer, lambda_init, tq, tk, cq):
    b, s, width = qa.shape
    heads = width // DA_VDIM
    assert DA_VDIM == LANES and s % tq == 0 and tq % (2 * tk) == 0 and tk % cq == 0
    assert va_t.shape == (b, heads, s // tk, DA_VDIM + BF16_ROWS, tk)
    k_spec = pl.BlockSpec((None, s, DA_VDIM), lambda bi, h, i: (bi, 0, h))
    vt_spec = pl.BlockSpec((None, None) + va_t.shape[2:], lambda bi, h, i: (bi, h, 0, 0, 0))
    stat = pltpu.VMEM((1, tq), F32)
    acc = pltpu.VMEM((DA_VDIM + BF16_ROWS, tq), F32)
    scores = pltpu.VMEM((2, tk, tq), F32)
    block_max = pltpu.VMEM((2, 1, tq), F32)
    return pl.pallas_call(
        functools.partial(_diffattn_kernel, lambda_init=lambda_init, tq=tq, tk=tk, cq=cq),
        grid=(b, heads, s // tq),
        in_specs=[pl.BlockSpec((None, tq, DA_VDIM), lambda bi, h, i: (bi, i, h)),
                  k_spec, vt_spec,
                  pl.BlockSpec((None, 4, HEAD_DIM), lambda bi, h, i: (layer, 0, 0)),
                  pl.BlockSpec((None, DA_VDIM, 1), lambda bi, h, i: (layer, 0, 0))],
        out_specs=pl.BlockSpec((None, tq, DA_VDIM), lambda bi, h, i: (bi, i, h)),
        out_shape=jax.ShapeDtypeStruct((b, s, width), BF16),
        scratch_shapes=[stat, acc, stat, acc, scores, block_max, scores, block_max],
        compiler_params=_params(3),
        name="diff_attention",
    )(qa, ka, va_t, lambda_qk, subln_col)


def _swa_kernel(q_ref, kc_ref, vc_ref, kp_ref, vp_ref, sink_ref, o_ref, *, tq):
    i = pl.program_id(1)
    kcat = jnp.concatenate([kp_ref[...], kc_ref[...]], axis=0)
    vcat = jnp.concatenate([vp_ref[...], vc_ref[...]], axis=0)
    n_r = tq // WINDOW
    rows_per = SW_GROUP * WINDOW
    stacked = (n_r * rows_per, 2 * WINDOW)
    row = lax.broadcasted_iota(jnp.int32, stacked, 0)
    qi = row % WINDOW
    kj = lax.broadcasted_iota(jnp.int32, stacked, 1)
    before_start = (row < rows_per) & (i == 0) & (kj < WINDOW)
    keep = (kj > qi) & (kj <= qi + WINDOW) & jnp.logical_not(before_start)
    bias = jnp.where(keep, 0.0, -jnp.inf).astype(F32)
    sinks = sink_ref[...]
    ones = jnp.ones((2 * WINDOW, LANES), BF16)
    pad = jnp.zeros((2 * WINDOW, LANES - HEAD_DIM), BF16)
    group_w = SW_GROUP * HEAD_DIM
    for kvh in range(SW_KV_HEADS):
        cols = slice(kvh * HEAD_DIM, (kvh + 1) * HEAD_DIM)
        sink_blk = jnp.concatenate(
            [jnp.broadcast_to(sinks[:, kvh * SW_GROUP + g:kvh * SW_GROUP + g + 1], (WINDOW, LANES))
             for g in range(SW_GROUP)], axis=0)
        sink_col = jnp.concatenate([sink_blk] * n_r, axis=0)
        scores = []
        for r in range(n_r):
            k = kcat[r * WINDOW:(r + 2) * WINDOW, cols]
            qblk = q_ref[r * WINDOW:(r + 1) * WINDOW, kvh * group_w:(kvh + 1) * group_w]
            q = jnp.concatenate([qblk[:, g * HEAD_DIM:(g + 1) * HEAD_DIM]
                                 for g in range(SW_GROUP)], axis=0)
            scores.append(lax.dot_general(q, k, NT_DIMS, preferred_element_type=F32))
        s = jnp.concatenate(scores, axis=0) + bias
        m = jnp.maximum(jnp.max(s, axis=1, keepdims=True), sink_col)
        p = jnp.exp(s - jnp.concatenate([m, m], axis=1)).astype(BF16)
        pvs = []
        for r in range(n_r):
            v = vcat[r * WINDOW:(r + 2) * WINDOW, cols]
            v_ones = jnp.concatenate([v, pad, ones], axis=1)
            pvs.append(jnp.dot(p[r * rows_per:(r + 1) * rows_per], v_ones,
                               preferred_element_type=F32))
        pv = jnp.concatenate(pvs, axis=0)
        denom = pv[:, LANES:] + jnp.exp(sink_col - m)
        o = (pv[:, :HEAD_DIM] / denom[:, :HEAD_DIM]).astype(BF16)
        for r in range(n_r):
            o_ref[r * WINDOW:(r + 1) * WINDOW, kvh * group_w:(kvh + 1) * group_w] = (
                jnp.concatenate([o[(r * SW_GROUP + g) * WINDOW:(r * SW_GROUP + g + 1) * WINDOW]
                                 for g in range(SW_GROUP)], axis=1))


def _swa_attention(qs, ksw, vsw, sinks, layer, tq):
    b, s, width = qs.shape
    kv_width = ksw.shape[-1]
    per = tq // WINDOW
    row = lambda bi, i: (bi, i, 0)
    prev = lambda bi, i: (bi, jnp.maximum(i * per - 1, 0), 0)
    return pl.pallas_call(
        functools.partial(_swa_kernel, tq=tq),
        grid=(b, s // tq),
        in_specs=[pl.BlockSpec((None, tq, width), row),
                  pl.BlockSpec((None, tq, kv_width), row),
                  pl.BlockSpec((None, tq, kv_width), row),
                  pl.BlockSpec((None, WINDOW, kv_width), prev),
                  pl.BlockSpec((None, WINDOW, kv_width), prev),
                  pl.BlockSpec((None, 1, SW_Q_HEADS), lambda bi, i: (layer, 0, 0))],
        out_specs=pl.BlockSpec((None, tq, width), row),
        out_shape=jax.ShapeDtypeStruct((b, s, width), BF16),
        compiler_params=_params(2),
        name="swa_attention",
    )(qs, ksw, vsw, ksw, vsw, sinks)


def _merge_kernel(x_ref, oa_ref, ob_ref, gate_ref, mod_ref, wa_ref, wb_ref, wo_ref, o_ref):
    d = x_ref.shape[-1]
    a = jnp.dot(oa_ref[...], wa_ref[...], preferred_element_type=F32)
    bb = jnp.dot(ob_ref[...], wb_ref[...], preferred_element_type=F32)
    mixed = gate_ref[:, :d].astype(F32) * a + gate_ref[:, d:].astype(F32) * bb
    y = jnp.dot(mixed.astype(BF16), wo_ref[...], preferred_element_type=F32)
    o_ref[...] = x_ref[...] + mod_ref[2:3, :] * y


def _merge_out(x, oa, ob, gates, mod_l, wa_bf, wb_bf, wo_bf, layer, tm):
    b, s, d = x.shape
    row = lambda bi, i: (bi, i, 0)
    wspec = lambda w: _resident((None,) + w.shape[1:], lambda bi, i: (layer, 0, 0))
    return pl.pallas_call(
        _merge_kernel,
        grid=(b, s // tm),
        in_specs=[pl.BlockSpec((None, tm, d), row),
                  pl.BlockSpec((None, tm, oa.shape[-1]), row),
                  pl.BlockSpec((None, tm, ob.shape[-1]), row),
                  pl.BlockSpec((None, tm, 2 * d), row),
                  pl.BlockSpec((None, N_MOD, d), lambda bi, i: (bi, 0, 0)),
                  wspec(wa_bf), wspec(wb_bf), wspec(wo_bf)],
        out_specs=pl.BlockSpec((None, tm, d), row),
        out_shape=jax.ShapeDtypeStruct((b, s, d), F32),
        compiler_params=_params(2),
        name="merge_out",
    )(x, oa, ob, gates, mod_l, wa_bf, wb_bf, wo_bf)


def _ffn_kernel(x_ref, halo_ref, mod_ref, g_ref, wg_ref, wu_ref, cw_ref, cb_ref, wd_ref, fin_ref,
                o_ref, h_ref, act_ref, *, chunk, last_layer):
    i = pl.program_id(1)
    mod = mod_ref[...]
    shift, scale, gate = mod[3:4], mod[4:5], mod[5:6]
    gain = g_ref[...]

    def normed(rows):
        return _rms(rows, NORM_EPS) * gain * (1.0 + scale) + shift

    x = x_ref[...]
    halo = jnp.where(i > 0, normed(halo_ref[...]), 0.0)
    h_ref[0:BF16_ROWS, :] = halo.astype(BF16)
    h_ref[BF16_ROWS:, :] = normed(x).astype(BF16)

    d_ff = wg_ref.shape[-1]
    for c0 in range(0, d_ff, chunk):
        cols = slice(c0, c0 + chunk)
        g = jnp.dot(h_ref[...], wg_ref[:, cols], preferred_element_type=F32)
        u = jnp.dot(h_ref[BF16_ROWS:, :], wu_ref[:, cols], preferred_element_type=F32)
        gc = cb_ref[:, cols] + cw_ref[CONV_W - 1:CONV_W, cols] * g[BF16_ROWS:]
        for t in range(CONV_W - 1):
            back = CONV_W - 1 - t
            gc = gc + cw_ref[t:t + 1, cols] * pltpu.roll(g, back, axis=0)[BF16_ROWS:]
        act_ref[:, cols] = ((gc * jax.nn.sigmoid(gc)) * u).astype(BF16)

    y = jnp.dot(act_ref[...], wd_ref[...], preferred_element_type=F32)
    out = x + gate * y
    if last_layer:
        out = _rms(out, NORM_EPS) * fin_ref[...]
    o_ref[...] = out


def _conv_ffn(x, mod_l, g, wg_bf, wu_bf, conv_w, conv_b, wd_bf, final_g, layer, tm, last_layer):
    b, s, d = x.shape
    d_ff = wg_bf.shape[-1]
    per = tm // BF16_ROWS
    row = lambda bi, i: (bi, i, 0)
    wspec = lambda w: _resident((None,) + w.shape[1:], lambda bi, i: (layer, 0, 0))
    return pl.pallas_call(
        functools.partial(_ffn_kernel, chunk=_ffn_chunk_width(d_ff), last_layer=last_layer),
        grid=(b, s // tm),
        in_specs=[pl.BlockSpec((None, tm, d), row),
                  pl.BlockSpec((None, BF16_ROWS, d),
                               lambda bi, i: (bi, jnp.maximum(i * per - 1, 0), 0)),
                  pl.BlockSpec((None, N_MOD, d), lambda bi, i: (bi, 0, 0)),
                  pl.BlockSpec((None, 1, d), lambda bi, i: (layer, 0, 0)),
                  wspec(wg_bf), wspec(wu_bf), wspec(conv_w), wspec(conv_b), wspec(wd_bf),
                  pl.BlockSpec((1, d), lambda bi, i: (0, 0))],
        out_specs=pl.BlockSpec((None, tm, d), row),
        out_shape=jax.ShapeDtypeStruct((b, s, d), F32),
        scratch_shapes=[pltpu.VMEM((BF16_ROWS + tm, d), BF16), pltpu.VMEM((tm, d_ff), BF16)],
        compiler_params=_params(2),
        name="conv_ffn",
    )(x, x, mod_l, g, wg_bf, wu_bf, conv_w, conv_b, wd_bf, final_g)


def _ffn_chunk_width(d_ff):
    for cw in (512, 384, 256, 128):
        if d_ff % cw == 0:
            return cw
    raise ValueError(f"d_ff={d_ff} is not a multiple of {LANES}")


def kernel(x, c, positions, w_ada, b_ada, attn_norm, w_in, lambda_qk, subln_norm, sinks,
           w_branch_a, w_branch_b, w_out, ffn_norm, w_gate, w_up, conv_w, conv_b, w_down,
           final_norm):
    b, s, d = x.shape
    depth = w_in.shape[0]
    d_ff = w_gate.shape[-1]
    tm = min(s, 512)
    tm_ffn = min(s, 1024)
    tq_da, tk_da, cq_da = min(s, 2048), 512, 256
    tq_sw = min(s, 512)

    mod = _adaln_mod(c, w_ada, b_ada)
    cos, sin_signed = _rope_tables(positions)

    w_in_bf = w_in.astype(BF16)
    wa_bf, wb_bf, wo_bf = (w.astype(BF16) for w in (w_branch_a, w_branch_b, w_out))
    wg_bf, wu_bf, wd_bf = (w.astype(BF16) for w in (w_gate, w_up, w_down))
    conv_b3 = conv_b.reshape(depth, 1, d_ff)
    attn_g = attn_norm.reshape(depth, 1, d)
    ffn_g = ffn_norm.reshape(depth, 1, d)
    subln_col = subln_norm.reshape(depth, DA_VDIM, 1)
    sinks3 = sinks.reshape(depth, 1, SW_Q_HEADS)
    final_g = final_norm.reshape(1, d)

    for l in range(depth):
        lambda_init = 0.8 - 0.6 * math.exp(-0.3 * l)
        mod_l = mod[l]
        qa, ka, va_t, qs, ksw, vsw, gates = _in_projection(x, mod_l, attn_g, cos, sin_signed,
                                                          w_in_bf, l, tk_da)
        oa = _diff_attention(qa, ka, va_t, lambda_qk, subln_col, l, lambda_init, tq_da, tk_da,
                             cq_da)
        ob = _swa_attention(qs, ksw, vsw, sinks3, l, tq_sw)
        x = _merge_out(x, oa, ob, gates, mod_l, wa_bf, wb_bf, wo_bf, l, tm_ffn)
        x = _conv_ffn(x, mod_l, ffn_g, wg_bf, wu_bf, conv_w, conv_b3, wd_bf, final_g, l, tm_ffn,
                      last_layer=(l == depth - 1))
    return x
```

```python
import functools
import math

import jax
import jax.numpy as jnp
from jax import lax
from jax.experimental import pallas as pl
from jax.experimental.pallas import tpu as pltpu

F32 = jnp.float32
BF16 = jnp.bfloat16

HEAD_DIM = 64
DA_HEADS = 8
DA_VDIM = 2 * HEAD_DIM
SW_Q_HEADS = 16
SW_KV_HEADS = 4
SW_GROUP = SW_Q_HEADS // SW_KV_HEADS
WINDOW = 128
CONV_W = 3
ROPE_THETA = 10000.0
NORM_EPS = 1e-6
SUBLN_EPS = 1e-5
N_MOD = 6
QK_SCALE = HEAD_DIM ** -0.5
LOG2_E = math.log2(math.e)

LANES = 128
BF16_ROWS = 16
VMEM_LIMIT = 56 * 1024 * 1024

NT_DIMS = (((1,), (1,)), ((), ()))


def _params(n_grid):
    return pltpu.CompilerParams(dimension_semantics=("arbitrary",) * n_grid,
                                vmem_limit_bytes=VMEM_LIMIT)


def _resident(shape, index_map):
    return pl.BlockSpec(shape, index_map, pipeline_mode=pl.Buffered(1))


def _rms(x, eps):
    return x * lax.rsqrt(jnp.mean(x * x, axis=-1, keepdims=True) + eps)


def _mod_kernel(c_ref, w_ref, b_ref, o_ref):
    c = c_ref[...]
    ca = c * jax.nn.sigmoid(c)
    o_ref[...] = jnp.dot(ca.astype(BF16), w_ref[...].astype(BF16),
                         preferred_element_type=F32) + b_ref[...]


def _adaln_mod(c, w_ada, b_ada):
    depth, d, width = w_ada.shape
    b = c.shape[0]
    rows = -(-b // 8) * 8
    c_pad = jnp.zeros((rows, d), F32).at[:b].set(c)
    tn = 1536
    out = pl.pallas_call(
        _mod_kernel,
        grid=(depth, width // tn),
        in_specs=[pl.BlockSpec((rows, d), lambda l, n: (0, 0)),
                  pl.BlockSpec((None, d, tn), lambda l, n: (l, 0, n)),
                  pl.BlockSpec((None, 1, tn), lambda l, n: (l, 0, n))],
        out_specs=pl.BlockSpec((None, rows, tn), lambda l, n: (l, 0, n)),
        out_shape=jax.ShapeDtypeStruct((depth, rows, width), F32),
        compiler_params=_params(2),
        name="adaln_mod",
    )(c_pad, w_ada, b_ada.reshape(depth, 1, width))
    return out[:, :b].reshape(depth, b, N_MOD, d)


def _rope_kernel(pos_ref, invf_ref, sign_ref, cos_ref, sin_ref):
    ang = pos_ref[...].astype(F32) * invf_ref[...]
    cos_ref[...] = jnp.cos(ang)
    sin_ref[...] = jnp.sin(ang) * sign_ref[...]


def _rope_tables(positions):
    b, s = positions.shape
    inv_freq = ROPE_THETA ** (-jnp.arange(0, HEAD_DIM, 2, dtype=F32) / HEAD_DIM)
    reps = LANES // (HEAD_DIM // 2)
    invf = jnp.tile(inv_freq, reps).reshape(1, LANES)
    half = HEAD_DIM // 2
    sign = jnp.tile(jnp.concatenate([-jnp.ones((half,), F32), jnp.ones((half,), F32)]),
                    LANES // HEAD_DIM).reshape(1, LANES)
    tm = min(s, 1024)
    return pl.pallas_call(
        _rope_kernel,
        grid=(b, s // tm),
        in_specs=[pl.BlockSpec((None, tm, 1), lambda bi, i: (bi, i, 0)),
                  pl.BlockSpec((1, LANES), lambda bi, i: (0, 0)),
                  pl.BlockSpec((1, LANES), lambda bi, i: (0, 0))],
        out_specs=[pl.BlockSpec((None, tm, LANES), lambda bi, i: (bi, i, 0))] * 2,
        out_shape=[jax.ShapeDtypeStruct((b, s, LANES), F32)] * 2,
        compiler_params=_params(2),
        name="rope_tables",
    )(positions.reshape(b, s, 1), invf, sign)


def _rope_block(a, cos, sin_signed, first_half):
    swapped = jnp.where(first_half, pltpu.roll(a, LANES - HEAD_DIM // 2, axis=1),
                        pltpu.roll(a, HEAD_DIM // 2, axis=1))
    return a * cos + swapped * sin_signed


def _inproj_kernel(x_ref, mod_ref, g_ref, cos_ref, sin_ref, w_ref,
                   qa_ref, ka_ref, va_ref, qs_ref, ks_ref, vs_ref, gate_ref, h_ref,
                   *, sections, chunk):
    x = x_ref[...]
    mod = mod_ref[...]
    shift, scale = mod[0:1], mod[1:2]
    h = _rms(x, NORM_EPS) * g_ref[...] * (1.0 + scale) + shift
    h_ref[...] = h.astype(BF16)

    cos = cos_ref[...]
    sin_signed = sin_ref[...]
    lane = lax.broadcasted_iota(jnp.int32, cos.shape, 1)
    first_half = (lane % HEAD_DIM) < (HEAD_DIM // 2)

    outs = (qa_ref, ka_ref, va_ref, qs_ref, ks_ref, vs_ref, gate_ref)
    col = 0
    for out_ref, (width, kind, mult) in zip(outs, sections):
        for c0 in range(0, width, chunk):
            cw = min(chunk, width - c0)
            acc = jnp.dot(h_ref[...], w_ref[:, col + c0:col + c0 + cw],
                          preferred_element_type=F32)
            if kind == "rope":
                for j in range(cw // LANES):
                    blk = _rope_block(acc[:, j * LANES:(j + 1) * LANES], cos, sin_signed,
                                      first_half)
                    if mult != 1.0:
                        blk = blk * mult
                    out_ref[:, c0 + j * LANES:c0 + (j + 1) * LANES] = blk.astype(BF16)
            elif kind == "sigmoid":
                out_ref[:, c0:c0 + cw] = jax.nn.sigmoid(acc).astype(BF16)
            elif kind == "value_t":
                for j in range(cw // DA_VDIM):
                    head = (c0 + j * DA_VDIM) // DA_VDIM
                    out_ref[head, 0:DA_VDIM, :] = acc[:, j * DA_VDIM:(j + 1) * DA_VDIM].T.astype(BF16)
                    out_ref[head, DA_VDIM:, :] = jnp.ones((BF16_ROWS, acc.shape[0]), BF16)
            else:
                out_ref[:, c0:c0 + cw] = acc.astype(BF16)
        col += width


def _in_projection(x, mod_l, g, cos, sin_signed, w_in_bf, layer, tm):
    b, s, d = x.shape
    da_qk = DA_HEADS * 2 * HEAD_DIM
    da_v = DA_HEADS * DA_VDIM
    sw_q = SW_Q_HEADS * HEAD_DIM
    sw_kv = SW_KV_HEADS * HEAD_DIM
    sections = ((da_qk, "rope", QK_SCALE * LOG2_E), (da_qk, "rope", 1.0), (da_v, "value_t", 1.0),
                (sw_q, "rope", QK_SCALE * LOG2_E), (sw_kv, "rope", 1.0), (sw_kv, "plain", 1.0),
                (2 * d, "sigmoid", 1.0))
    widths = [w for w, _, _ in sections]
    in_width = sum(widths)
    assert w_in_bf.shape[1:] == (d, in_width)
    row = lambda bi, i: (bi, i, 0)
    vt_rows = DA_VDIM + BF16_ROWS
    vt_spec = pl.BlockSpec((None, DA_HEADS, None, vt_rows, tm), lambda bi, i: (bi, 0, i, 0, 0))
    vt_shape = jax.ShapeDtypeStruct((b, DA_HEADS, s // tm, vt_rows, tm), BF16)
    return pl.pallas_call(
        functools.partial(_inproj_kernel, sections=sections, chunk=512),
        grid=(b, s // tm),
        in_specs=[pl.BlockSpec((None, tm, d), row),
                  pl.BlockSpec((None, N_MOD, d), lambda bi, i: (bi, 0, 0)),
                  pl.BlockSpec((None, 1, d), lambda bi, i: (layer, 0, 0)),
                  pl.BlockSpec((None, tm, LANES), row),
                  pl.BlockSpec((None, tm, LANES), row),
                  _resident((None, d, in_width), lambda bi, i: (layer, 0, 0))],
        out_specs=[vt_spec if kind == "value_t" else pl.BlockSpec((None, tm, w), row)
                   for w, kind, _ in sections],
        out_shape=[vt_shape if kind == "value_t" else jax.ShapeDtypeStruct((b, s, w), BF16)
                   for w, kind, _ in sections],
        scratch_shapes=[pltpu.VMEM((tm, d), BF16)],
        compiler_params=_params(2),
        name="in_projection",
    )(x, mod_l, g, cos, sin_signed, w_in_bf)


def _diffattn_kernel(q_ref, k_ref, vt_ref, lam_ref, g_ref, o_ref,
                     m0_ref, a0_ref, m1_ref, a1_ref, sa_ref, xa_ref, sb_ref, xb_ref,
                     *, lambda_init, tq, tk, cq):
    i = pl.program_id(2)
    diag_blocks = tq // tk

    q = q_ref[...]
    lane = lax.broadcasted_iota(jnp.int32, q.shape, 1)
    zero = jnp.zeros_like(q)
    halves = ((jnp.where(lane < HEAD_DIM, q, zero), m0_ref, a0_ref),
              (jnp.where(lane >= HEAD_DIM, q, zero), m1_ref, a1_ref))
    for _, m_ref, a_ref in halves:
        m_ref[...] = jnp.full(m_ref.shape, -jnp.inf, F32)
        a_ref[...] = jnp.zeros(a_ref.shape, F32)

    def live_keys(diag_off):
        return tk if diag_off is None else min(tk, diag_off + cq)

    def produce(buf, j, c0, diag_off, h):
        s_ref, x_ref = buf
        n_keys = live_keys(diag_off)
        k = k_ref[pl.ds(pl.multiple_of(j * tk, tk), n_keys), :]
        s_t = lax.dot_general(k, halves[h][0][c0:c0 + cq], NT_DIMS, preferred_element_type=F32)
        if diag_off is not None:
            key = lax.broadcasted_iota(jnp.int32, s_t.shape, 0)
            query = lax.broadcasted_iota(jnp.int32, s_t.shape, 1) + diag_off
            s_t = jnp.where(key <= query, s_t, -jnp.inf)
        s_ref[h, :n_keys, c0:c0 + cq] = s_t
        x_ref[h, :, c0:c0 + cq] = jnp.max(s_t, axis=0, keepdims=True)

    def consume(buf, j, c0, diag_off, h):
        s_ref, x_ref = buf
        _, m_ref, a_ref = halves[h]
        n_keys = live_keys(diag_off)
        m_old = m_ref[:, c0:c0 + cq]
        m_new = jnp.maximum(m_old, x_ref[h, :, c0:c0 + cq])
        alpha = jnp.exp2(m_old - m_new)
        p_t = jnp.exp2(s_ref[h, :n_keys, c0:c0 + cq] - m_new).astype(BF16)
        a_ref[:, c0:c0 + cq] = alpha * a_ref[:, c0:c0 + cq] + jnp.dot(
            vt_ref[j, :, :n_keys], p_t, preferred_element_type=F32)
        m_ref[:, c0:c0 + cq] = m_new

    def chunks(diag):
        if diag is None:
            return [(c0, None) for c0 in range(0, tq, cq)]
        first = diag * tk
        return [(c0, c0 - first if c0 < first + tk else None) for c0 in range(first, tq, cq)]

    def overlap(produced, consumed):
        puts = [] if produced is None else [
            (produced[0], produced[1], c0, off) for c0, off in chunks(produced[2])]
        gets = [] if consumed is None else [
            (consumed[0], consumed[1], c0, off) for c0, off in chunks(consumed[2])]
        for n in range(max(len(puts), len(gets))):
            for h in range(len(halves)):
                if n < len(gets):
                    consume(*gets[n], h)
                if n < len(puts):
                    produce(*puts[n], h)

    buf_a, buf_b = (sa_ref, xa_ref), (sb_ref, xb_ref)
    n_full = i * diag_blocks

    @pl.when(i == 0)
    def _():
        overlap((buf_a, 0, 0), None)

    @pl.when(i > 0)
    def _():
        overlap((buf_a, 0, None), None)

    def pair(t, carry):
        j = 2 * t
        overlap((buf_b, j + 1, None), (buf_a, j, None))
        overlap((buf_a, j + 2, None), (buf_b, j + 1, None))
        return carry

    lax.fori_loop(0, n_full // 2 - 1, pair, 0)

    @pl.when(i > 0)
    def _():
        j = n_full - 2
        overlap((buf_b, j + 1, None), (buf_a, j, None))
        overlap((buf_a, j + 2, 0), (buf_b, j + 1, None))

    bufs = (buf_a, buf_b)
    for d in range(diag_blocks):
        following = (bufs[(d + 1) % 2], n_full + d + 1, d + 1) if d + 1 < diag_blocks else None
        overlap(following, (bufs[d % 2], n_full + d, d))

    lq = lam_ref[...]
    lam = (jnp.exp(jnp.sum(lq[0:1] * lq[1:2], axis=1, keepdims=True))
           - jnp.exp(jnp.sum(lq[2:3] * lq[3:4], axis=1, keepdims=True)) + lambda_init)
    a0 = a0_ref[...]
    a1 = a1_ref[...]
    o_t = (a0[:DA_VDIM] / a0[DA_VDIM:DA_VDIM + 1]
           - lam * (a1[:DA_VDIM] / a1[DA_VDIM:DA_VDIM + 1]))
    o_t = o_t * lax.rsqrt(jnp.mean(o_t * o_t, axis=0, keepdims=True) + SUBLN_EPS)
    o_t = o_t * g_ref[...] * (1.0 - lambda_init)
    o_ref[...] = o_t.T.astype(BF16)


def _diff_attention(qa, ka, va_t, lambda_qk, subln_col, layer, lambda_init, tq, tk, cq):
    b, s, width = qa.shape
    heads = width // DA_VDIM
    assert DA_VDIM == LANES and s % tq == 0 and tq % (2 * tk) == 0 and tk % cq == 0
    assert va_t.shape == (b, heads, s // tk, DA_VDIM + BF16_ROWS, tk)
    k_spec = pl.BlockSpec((None, s, DA_VDIM), lambda bi, h, i: (bi, 0, h))
    vt_spec = pl.BlockSpec((None, None) + va_t.shape[2:], lambda bi, h, i: (bi, h, 0, 0, 0))
    stat = pltpu.VMEM((1, tq), F32)
    acc = pltpu.VMEM((DA_VDIM + BF16_ROWS, tq), F32)
    scores = pltpu.VMEM((2, tk, tq), F32)
    block_max = pltpu.VMEM((2, 1, tq), F32)
    return pl.pallas_call(
        functools.partial(_diffattn_kernel, lambda_init=lambda_init, tq=tq, tk=tk, cq=cq),
        grid=(b, heads, s // tq),
        in_specs=[pl.BlockSpec((None, tq, DA_VDIM), lambda bi, h, i: (bi, i, h)),
                  k_spec, vt_spec,
                  pl.BlockSpec((None, 4, HEAD_DIM), lambda bi, h, i: (layer, 0, 0)),
                  pl.BlockSpec((None, DA_VDIM, 1), lambda bi, h, i: (layer, 0, 0))],
        out_specs=pl.BlockSpec((None, tq, DA_VDIM), lambda bi, h, i: (bi, i, h)),
        out_shape=jax.ShapeDtypeStruct((b, s, width), BF16),
        scratch_shapes=[stat, acc, stat, acc, scores, block_max, scores, block_max],
        compiler_params=_params(3),
        name="diff_attention",
    )(qa, ka, va_t, lambda_qk, subln_col)


def _swa_kernel(q_ref, kc_ref, vc_ref, kp_ref, vp_ref, sink_ref, o_ref, *, tq):
    i = pl.program_id(1)
    kcat = jnp.concatenate([kp_ref[...], kc_ref[...]], axis=0)
    vcat = jnp.concatenate([vp_ref[...], vc_ref[...]], axis=0)
    n_r = tq // WINDOW
    rows_per = SW_GROUP * WINDOW
    stacked = (n_r * rows_per, 2 * WINDOW)
    row = lax.broadcasted_iota(jnp.int32, stacked, 0)
    qi = row % WINDOW
    kj = lax.broadcasted_iota(jnp.int32, stacked, 1)
    before_start = (row < rows_per) & (i == 0) & (kj < WINDOW)
    keep = (kj > qi) & (kj <= qi + WINDOW) & jnp.logical_not(before_start)
    bias = jnp.where(keep, 0.0, -jnp.inf).astype(F32)
    bias_lo, bias_hi = bias[:, :LANES], bias[:, LANES:]
    sink_lane = lax.broadcasted_iota(jnp.int32, (n_r * rows_per, LANES), 1) == 0
    key0 = lax.broadcasted_iota(jnp.int32, (2 * WINDOW, HEAD_DIM), 0) == 0
    zero_kv = jnp.zeros((2 * WINDOW, HEAD_DIM), BF16)
    sinks = sink_ref[...] * LOG2_E
    ones = jnp.ones((2 * WINDOW, LANES), BF16)
    pad = jnp.zeros((2 * WINDOW, LANES - HEAD_DIM), BF16)
    group_w = SW_GROUP * HEAD_DIM
    for kvh in range(SW_KV_HEADS):
        cols = slice(kvh * HEAD_DIM, (kvh + 1) * HEAD_DIM)
        sink_blk = jnp.concatenate(
            [jnp.broadcast_to(sinks[:, kvh * SW_GROUP + g:kvh * SW_GROUP + g + 1], (WINDOW, LANES))
             for g in range(SW_GROUP)], axis=0)
        sink_col = jnp.concatenate([sink_blk] * n_r, axis=0)
        bias_k = jnp.concatenate([jnp.where(sink_lane, sink_col, bias_lo), bias_hi], axis=1)
        scores = []
        for r in range(n_r):
            k = jnp.where(key0, zero_kv, kcat[r * WINDOW:(r + 2) * WINDOW, cols])
            qblk = q_ref[r * WINDOW:(r + 1) * WINDOW, kvh * group_w:(kvh + 1) * group_w]
            q = jnp.concatenate([qblk[:, g * HEAD_DIM:(g + 1) * HEAD_DIM]
                                 for g in range(SW_GROUP)], axis=0)
            scores.append(lax.dot_general(q, k, NT_DIMS, preferred_element_type=F32))
        s = jnp.concatenate(scores, axis=0) + bias_k
        m = jnp.broadcast_to(jnp.max(s, axis=1, keepdims=True), (n_r * rows_per, LANES))
        p = jnp.exp2(s - jnp.concatenate([m, m], axis=1)).astype(BF16)
        pvs = []
        for r in range(n_r):
            v = jnp.where(key0, zero_kv, vcat[r * WINDOW:(r + 2) * WINDOW, cols])
            v_ones = jnp.concatenate([v, pad, ones], axis=1)
            pvs.append(jnp.dot(p[r * rows_per:(r + 1) * rows_per], v_ones,
                               preferred_element_type=F32))
        pv = jnp.concatenate(pvs, axis=0)
        o = (pv[:, :HEAD_DIM] / pv[:, LANES:LANES + HEAD_DIM]).astype(BF16)
        for r in range(n_r):
            o_ref[r * WINDOW:(r + 1) * WINDOW, kvh * group_w:(kvh + 1) * group_w] = (
                jnp.concatenate([o[(r * SW_GROUP + g) * WINDOW:(r * SW_GROUP + g + 1) * WINDOW]
                                 for g in range(SW_GROUP)], axis=1))


def _swa_attention(qs, ksw, vsw, sinks, layer, tq):
    b, s, width = qs.shape
    kv_width = ksw.shape[-1]
    per = tq // WINDOW
    row = lambda bi, i: (bi, i, 0)
    prev = lambda bi, i: (bi, jnp.maximum(i * per - 1, 0), 0)
    return pl.pallas_call(
        functools.partial(_swa_kernel, tq=tq),
        grid=(b, s // tq),
        in_specs=[pl.BlockSpec((None, tq, width), row),
                  pl.BlockSpec((None, tq, kv_width), row),
                  pl.BlockSpec((None, tq, kv_width), row),
                  pl.BlockSpec((None, WINDOW, kv_width), prev),
                  pl.BlockSpec((None, WINDOW, kv_width), prev),
                  pl.BlockSpec((None, 1, SW_Q_HEADS), lambda bi, i: (layer, 0, 0))],
        out_specs=pl.BlockSpec((None, tq, width), row),
        out_shape=jax.ShapeDtypeStruct((b, s, width), BF16),
        compiler_params=_params(2),
        name="swa_attention",
    )(qs, ksw, vsw, ksw, vsw, sinks)


def _merge_kernel(x_ref, oa_ref, ob_ref, gate_ref, mod_ref, wa_ref, wb_ref, wo_ref, o_ref):
    d = x_ref.shape[-1]
    a = jnp.dot(oa_ref[...], wa_ref[...], preferred_element_type=F32)
    bb = jnp.dot(ob_ref[...], wb_ref[...], preferred_element_type=F32)
    mixed = gate_ref[:, :d].astype(F32) * a + gate_ref[:, d:].astype(F32) * bb
    y = jnp.dot(mixed.astype(BF16), wo_ref[...], preferred_element_type=F32)
    o_ref[...] = x_ref[...] + mod_ref[2:3, :] * y


def _merge_out(x, oa, ob, gates, mod_l, wa_bf, wb_bf, wo_bf, layer, tm):
    b, s, d = x.shape
    row = lambda bi, i: (bi, i, 0)
    wspec = lambda w: _resident((None,) + w.shape[1:], lambda bi, i: (layer, 0, 0))
    return pl.pallas_call(
        _merge_kernel,
        grid=(b, s // tm),
        in_specs=[pl.BlockSpec((None, tm, d), row),
                  pl.BlockSpec((None, tm, oa.shape[-1]), row),
                  pl.BlockSpec((None, tm, ob.shape[-1]), row),
                  pl.BlockSpec((None, tm, 2 * d), row),
                  pl.BlockSpec((None, N_MOD, d), lambda bi, i: (bi, 0, 0)),
                  wspec(wa_bf), wspec(wb_bf), wspec(wo_bf)],
        out_specs=pl.BlockSpec((None, tm, d), row),
        out_shape=jax.ShapeDtypeStruct((b, s, d), F32),
        compiler_params=_params(2),
        name="merge_out",
    )(x, oa, ob, gates, mod_l, wa_bf, wb_bf, wo_bf)


def _ffn_kernel(x_ref, halo_ref, mod_ref, g_ref, wg_ref, wu_ref, cw_ref, cb_ref, wd_ref, fin_ref,
                o_ref, h_ref, act_ref, *, chunk, last_layer):
    i = pl.program_id(1)
    mod = mod_ref[...]
    shift, scale, gate = mod[3:4], mod[4:5], mod[5:6]
    gain = g_ref[...]

    def normed(rows):
        return _rms(rows, NORM_EPS) * gain * (1.0 + scale) + shift

    x = x_ref[...]
    halo = jnp.where(i > 0, normed(halo_ref[...]), 0.0)
    h_ref[0:BF16_ROWS, :] = halo.astype(BF16)
    h_ref[BF16_ROWS:, :] = normed(x).astype(BF16)

    d_ff = wg_ref.shape[-1]
    for c0 in range(0, d_ff, chunk):
        cols = slice(c0, c0 + chunk)
        g = jnp.dot(h_ref[...], wg_ref[:, cols], preferred_element_type=F32)
        u = jnp.dot(h_ref[BF16_ROWS:, :], wu_ref[:, cols], preferred_element_type=F32)
        gc = cb_ref[:, cols] + cw_ref[CONV_W - 1:CONV_W, cols] * g[BF16_ROWS:]
        for t in range(CONV_W - 1):
            back = CONV_W - 1 - t
            gc = gc + cw_ref[t:t + 1, cols] * pltpu.roll(g, back, axis=0)[BF16_ROWS:]
        act_ref[:, cols] = ((gc * jax.nn.sigmoid(gc)) * u).astype(BF16)

    y = jnp.dot(act_ref[...], wd_ref[...], preferred_element_type=F32)
    out = x + gate * y
    if last_layer:
        out = _rms(out, NORM_EPS) * fin_ref[...]
    o_ref[...] = out


def _conv_ffn(x, mod_l, g, wg_bf, wu_bf, conv_w, conv_b, wd_bf, final_g, layer, tm, last_layer):
    b, s, d = x.shape
    d_ff = wg_bf.shape[-1]
    per = tm // BF16_ROWS
    row = lambda bi, i: (bi, i, 0)
    wspec = lambda w: _resident((None,) + w.shape[1:], lambda bi, i: (layer, 0, 0))
    return pl.pallas_call(
        functools.partial(_ffn_kernel, chunk=_ffn_chunk_width(d_ff), last_layer=last_layer),
        grid=(b, s // tm),
        in_specs=[pl.BlockSpec((None, tm, d), row),
                  pl.BlockSpec((None, BF16_ROWS, d),
                               lambda bi, i: (bi, jnp.maximum(i * per - 1, 0), 0)),
                  pl.BlockSpec((None, N_MOD, d), lambda bi, i: (bi, 0, 0)),
                  pl.BlockSpec((None, 1, d), lambda bi, i: (layer, 0, 0)),
                  wspec(wg_bf), wspec(wu_bf), wspec(conv_w), wspec(conv_b), wspec(wd_bf),
                  pl.BlockSpec((1, d), lambda bi, i: (0, 0))],
        out_specs=pl.BlockSpec((None, tm, d), row),
        out_shape=jax.ShapeDtypeStruct((b, s, d), F32),
        scratch_shapes=[pltpu.VMEM((BF16_ROWS + tm, d), BF16), pltpu.VMEM((tm, d_ff), BF16)],
        compiler_params=_params(2),
        name="conv_ffn",
    )(x, x, mod_l, g, wg_bf, wu_bf, conv_w, conv_b, wd_bf, final_g)


def _ffn_chunk_width(d_ff):
    for cw in (512, 384, 256, 128):
        if d_ff % cw == 0:
            return cw
    raise ValueError(f"d_ff={d_ff} is not a multiple of {LANES}")


def kernel(x, c, positions, w_ada, b_ada, attn_norm, w_in, lambda_qk, subln_norm, sinks,
           w_branch_a, w_branch_b, w_out, ffn_norm, w_gate, w_up, conv_w, conv_b, w_down,
           final_norm):
    b, s, d = x.shape
    depth = w_in.shape[0]
    d_ff = w_gate.shape[-1]
    tm = min(s, 512)
    tm_ffn = min(s, 1024)
    tq_da, tk_da, cq_da = min(s, 2048), 512, 256
    tq_sw = min(s, 512)

    mod = _adaln_mod(c, w_ada, b_ada)
    cos, sin_signed = _rope_tables(positions)

    w_in_bf = w_in.astype(BF16)
    wa_bf, wb_bf, wo_bf = (w.astype(BF16) for w in (w_branch_a, w_branch_b, w_out))
    wg_bf, wu_bf, wd_bf = (w.astype(BF16) for w in (w_gate, w_up, w_down))
    conv_b3 = conv_b.reshape(depth, 1, d_ff)
    attn_g = attn_norm.reshape(depth, 1, d)
    ffn_g = ffn_norm.reshape(depth, 1, d)
    subln_col = subln_norm.reshape(depth, DA_VDIM, 1)
    sinks3 = sinks.reshape(depth, 1, SW_Q_HEADS)
    final_g = final_norm.reshape(1, d)

    for l in range(depth):
        lambda_init = 0.8 - 0.6 * math.exp(-0.3 * l)
        mod_l = mod[l]
        qa, ka, va_t, qs, ksw, vsw, gates = _in_projection(x, mod_l, attn_g, cos, sin_signed,
                                                          w_in_bf, l, tk_da)
        oa = _diff_attention(qa, ka, va_t, lambda_qk, subln_col, l, lambda_init, tq_da, tk_da,
                             cq_da)
        ob = _swa_attention(qs, ksw, vsw, sinks3, l, tq_sw)
        x = _merge_out(x, oa, ob, gates, mod_l, wa_bf, wb_bf, wo_bf, l, tm_ffn)
        x = _conv_ffn(x, mod_l, ffn_g, wg_bf, wu_bf, conv_w, conv_b3, wd_bf, final_g, l, tm_ffn,
                      last_layer=(l == depth - 1))
    return x
```

```python
import functools
import math

import jax
import jax.numpy as jnp
from jax import lax
from jax.experimental import pallas as pl
from jax.experimental.pallas import tpu as pltpu

F32 = jnp.float32
BF16 = jnp.bfloat16

HEAD_DIM = 64
DA_HEADS = 8
DA_VDIM = 2 * HEAD_DIM
SW_Q_HEADS = 16
SW_KV_HEADS = 4
SW_GROUP = SW_Q_HEADS // SW_KV_HEADS
WINDOW = 128
CONV_W = 3
ROPE_THETA = 10000.0
NORM_EPS = 1e-6
SUBLN_EPS = 1e-5
N_MOD = 6
QK_SCALE = HEAD_DIM ** -0.5
LOG2_E = math.log2(math.e)

LANES = 128
BF16_ROWS = 16
VMEM_LIMIT = 56 * 1024 * 1024

NT_DIMS = (((1,), (1,)), ((), ()))


def _params(n_grid):
    return pltpu.CompilerParams(dimension_semantics=("arbitrary",) * n_grid,
                                vmem_limit_bytes=VMEM_LIMIT)


def _resident(shape, index_map):
    return pl.BlockSpec(shape, index_map, pipeline_mode=pl.Buffered(1))


def _rms(x, eps):
    return x * lax.rsqrt(jnp.mean(x * x, axis=-1, keepdims=True) + eps)


def _mod_kernel(c_ref, w_ref, b_ref, o_ref):
    c = c_ref[...]
    ca = c * jax.nn.sigmoid(c)
    o_ref[...] = jnp.dot(ca.astype(BF16), w_ref[...].astype(BF16),
                         preferred_element_type=F32) + b_ref[...]


def _adaln_mod(c, w_ada, b_ada):
    depth, d, width = w_ada.shape
    b = c.shape[0]
    rows = -(-b // 8) * 8
    c_pad = jnp.zeros((rows, d), F32).at[:b].set(c)
    tn = 1536
    out = pl.pallas_call(
        _mod_kernel,
        grid=(depth, width // tn),
        in_specs=[pl.BlockSpec((rows, d), lambda l, n: (0, 0)),
                  pl.BlockSpec((None, d, tn), lambda l, n: (l, 0, n)),
                  pl.BlockSpec((None, 1, tn), lambda l, n: (l, 0, n))],
        out_specs=pl.BlockSpec((None, rows, tn), lambda l, n: (l, 0, n)),
        out_shape=jax.ShapeDtypeStruct((depth, rows, width), F32),
        compiler_params=_params(2),
        name="adaln_mod",
    )(c_pad, w_ada, b_ada.reshape(depth, 1, width))
    return out[:, :b].reshape(depth, b, N_MOD, d)


def _rope_kernel(pos_ref, invf_ref, sign_ref, cos_ref, sin_ref):
    ang = pos_ref[...].astype(F32) * invf_ref[...]
    cos_ref[...] = jnp.cos(ang)
    sin_ref[...] = jnp.sin(ang) * sign_ref[...]


def _rope_tables(positions):
    b, s = positions.shape
    inv_freq = ROPE_THETA ** (-jnp.arange(0, HEAD_DIM, 2, dtype=F32) / HEAD_DIM)
    reps = LANES // (HEAD_DIM // 2)
    invf = jnp.tile(inv_freq, reps).reshape(1, LANES)
    half = HEAD_DIM // 2
    sign = jnp.tile(jnp.concatenate([-jnp.ones((half,), F32), jnp.ones((half,), F32)]),
                    LANES // HEAD_DIM).reshape(1, LANES)
    tm = min(s, 1024)
    return pl.pallas_call(
        _rope_kernel,
        grid=(b, s // tm),
        in_specs=[pl.BlockSpec((None, tm, 1), lambda bi, i: (bi, i, 0)),
                  pl.BlockSpec((1, LANES), lambda bi, i: (0, 0)),
                  pl.BlockSpec((1, LANES), lambda bi, i: (0, 0))],
        out_specs=[pl.BlockSpec((None, tm, LANES), lambda bi, i: (bi, i, 0))] * 2,
        out_shape=[jax.ShapeDtypeStruct((b, s, LANES), F32)] * 2,
        compiler_params=_params(2),
        name="rope_tables",
    )(positions.reshape(b, s, 1), invf, sign)


def _rope_block(a, cos, sin_signed, first_half):
    swapped = jnp.where(first_half, pltpu.roll(a, LANES - HEAD_DIM // 2, axis=1),
                        pltpu.roll(a, HEAD_DIM // 2, axis=1))
    return a * cos + swapped * sin_signed


def _inproj_kernel(x_ref, mod_ref, g_ref, cos_ref, sin_ref, w_ref,
                   qa_ref, ka_ref, va_ref, qs_ref, ks_ref, vs_ref, gate_ref, h_ref,
                   *, sections, chunk):
    x = x_ref[...]
    mod = mod_ref[...]
    shift, scale = mod[0:1], mod[1:2]
    h = _rms(x, NORM_EPS) * g_ref[...] * (1.0 + scale) + shift
    h_ref[...] = h.astype(BF16)

    cos = cos_ref[...]
    sin_signed = sin_ref[...]
    lane = lax.broadcasted_iota(jnp.int32, cos.shape, 1)
    first_half = (lane % HEAD_DIM) < (HEAD_DIM // 2)

    outs = (qa_ref, ka_ref, va_ref, qs_ref, ks_ref, vs_ref, gate_ref)
    col = 0
    for out_ref, (width, kind, mult) in zip(outs, sections):
        for c0 in range(0, width, chunk):
            cw = min(chunk, width - c0)
            acc = jnp.dot(h_ref[...], w_ref[:, col + c0:col + c0 + cw],
                          preferred_element_type=F32)
            if kind == "rope":
                for j in range(cw // LANES):
                    blk = _rope_block(acc[:, j * LANES:(j + 1) * LANES], cos, sin_signed,
                                      first_half)
                    if mult != 1.0:
                        blk = blk * mult
                    out_ref[:, c0 + j * LANES:c0 + (j + 1) * LANES] = blk.astype(BF16)
            elif kind == "sigmoid":
                out_ref[:, c0:c0 + cw] = jax.nn.sigmoid(acc).astype(BF16)
            elif kind == "value_t":
                for j in range(cw // DA_VDIM):
                    head = (c0 + j * DA_VDIM) // DA_VDIM
                    out_ref[head, 0:DA_VDIM, :] = acc[:, j * DA_VDIM:(j + 1) * DA_VDIM].T.astype(BF16)
                    out_ref[head, DA_VDIM:, :] = jnp.ones((BF16_ROWS, acc.shape[0]), BF16)
            else:
                out_ref[:, c0:c0 + cw] = acc.astype(BF16)
        col += width


def _in_projection(x, mod_l, g, cos, sin_signed, w_in_bf, layer, tm):
    b, s, d = x.shape
    da_qk = DA_HEADS * 2 * HEAD_DIM
    da_v = DA_HEADS * DA_VDIM
    sw_q = SW_Q_HEADS * HEAD_DIM
    sw_kv = SW_KV_HEADS * HEAD_DIM
    sections = ((da_qk, "rope", QK_SCALE * LOG2_E), (da_qk, "rope", 1.0), (da_v, "value_t", 1.0),
                (sw_q, "rope", QK_SCALE * LOG2_E), (sw_kv, "rope", 1.0), (sw_kv, "plain", 1.0),
                (2 * d, "sigmoid", 1.0))
    widths = [w for w, _, _ in sections]
    in_width = sum(widths)
    assert w_in_bf.shape[1:] == (d, in_width)
    row = lambda bi, i: (bi, i, 0)
    vt_rows = DA_VDIM + BF16_ROWS
    vt_spec = pl.BlockSpec((None, DA_HEADS, None, vt_rows, tm), lambda bi, i: (bi, 0, i, 0, 0))
    vt_shape = jax.ShapeDtypeStruct((b, DA_HEADS, s // tm, vt_rows, tm), BF16)
    return pl.pallas_call(
        functools.partial(_inproj_kernel, sections=sections, chunk=512),
        grid=(b, s // tm),
        in_specs=[pl.BlockSpec((None, tm, d), row),
                  pl.BlockSpec((None, N_MOD, d), lambda bi, i: (bi, 0, 0)),
                  pl.BlockSpec((None, 1, d), lambda bi, i: (layer, 0, 0)),
                  pl.BlockSpec((None, tm, LANES), row),
                  pl.BlockSpec((None, tm, LANES), row),
                  _resident((None, d, in_width), lambda bi, i: (layer, 0, 0))],
        out_specs=[vt_spec if kind == "value_t" else pl.BlockSpec((None, tm, w), row)
                   for w, kind, _ in sections],
        out_shape=[vt_shape if kind == "value_t" else jax.ShapeDtypeStruct((b, s, w), BF16)
                   for w, kind, _ in sections],
        scratch_shapes=[pltpu.VMEM((tm, d), BF16)],
        compiler_params=_params(2),
        name="in_projection",
    )(x, mod_l, g, cos, sin_signed, w_in_bf)


def _diffattn_kernel(q_ref, k_ref, vt_ref, lam_ref, g_ref, o_ref,
                     m0_ref, a0_ref, m1_ref, a1_ref, sa_ref, xa_ref, sb_ref, xb_ref,
                     *, lambda_init, tq, tk, cq):
    i = pl.program_id(2)
    diag_blocks = tq // tk

    q = q_ref[...]
    lane = lax.broadcasted_iota(jnp.int32, q.shape, 1)
    zero = jnp.zeros_like(q)
    halves = ((jnp.where(lane < HEAD_DIM, q, zero), m0_ref, a0_ref),
              (jnp.where(lane >= HEAD_DIM, q, zero), m1_ref, a1_ref))
    for _, m_ref, a_ref in halves:
        m_ref[...] = jnp.full(m_ref.shape, -jnp.inf, F32)
        a_ref[...] = jnp.zeros(a_ref.shape, F32)

    def live_keys(diag_off):
        return tk if diag_off is None else min(tk, diag_off + cq)

    def produce(buf, j, c0, diag_off, h):
        s_ref, x_ref = buf
        n_keys = live_keys(diag_off)
        k = k_ref[pl.ds(pl.multiple_of(j * tk, tk), n_keys), :]
        s_t = lax.dot_general(k, halves[h][0][c0:c0 + cq], NT_DIMS, preferred_element_type=F32)
        if diag_off is not None:
            key = lax.broadcasted_iota(jnp.int32, s_t.shape, 0)
            query = lax.broadcasted_iota(jnp.int32, s_t.shape, 1) + diag_off
            s_t = jnp.where(key <= query, s_t, -jnp.inf)
        s_ref[h, :n_keys, c0:c0 + cq] = s_t
        x_ref[h, :, c0:c0 + cq] = jnp.max(s_t, axis=0, keepdims=True)

    def consume(buf, j, c0, diag_off, h):
        s_ref, x_ref = buf
        _, m_ref, a_ref = halves[h]
        n_keys = live_keys(diag_off)
        m_old = m_ref[:, c0:c0 + cq]
        m_new = jnp.maximum(m_old, x_ref[h, :, c0:c0 + cq])
        alpha = jnp.exp2(m_old - m_new)
        p_t = jnp.exp2(s_ref[h, :n_keys, c0:c0 + cq] - m_new).astype(BF16)
        a_ref[:, c0:c0 + cq] = alpha * a_ref[:, c0:c0 + cq] + jnp.dot(
            vt_ref[j, :, :n_keys], p_t, preferred_element_type=F32)
        m_ref[:, c0:c0 + cq] = m_new

    def chunks(diag):
        if diag is None:
            return [(c0, None) for c0 in range(0, tq, cq)]
        first = diag * tk
        return [(c0, c0 - first if c0 < first + tk else None) for c0 in range(first, tq, cq)]

    def overlap(produced, consumed):
        puts = [] if produced is None else [
            (produced[0], produced[1], c0, off) for c0, off in chunks(produced[2])]
        gets = [] if consumed is None else [
            (consumed[0], consumed[1], c0, off) for c0, off in chunks(consumed[2])]
        for n in range(max(len(puts), len(gets))):
            for h in range(len(halves)):
                if n < len(gets):
                    consume(*gets[n], h)
                if n < len(puts):
                    produce(*puts[n], h)

    buf_a, buf_b = (sa_ref, xa_ref), (sb_ref, xb_ref)
    n_full = i * diag_blocks

    @pl.when(i == 0)
    def _():
        overlap((buf_a, 0, 0), None)

    @pl.when(i > 0)
    def _():
        overlap((buf_a, 0, None), None)

    def pair(t, carry):
        j = 2 * t
        overlap((buf_b, j + 1, None), (buf_a, j, None))
        overlap((buf_a, j + 2, None), (buf_b, j + 1, None))
        return carry

    lax.fori_loop(0, n_full // 2 - 1, pair, 0)

    @pl.when(i > 0)
    def _():
        j = n_full - 2
        overlap((buf_b, j + 1, None), (buf_a, j, None))
        overlap((buf_a, j + 2, 0), (buf_b, j + 1, None))

    bufs = (buf_a, buf_b)
    for d in range(diag_blocks):
        following = (bufs[(d + 1) % 2], n_full + d + 1, d + 1) if d + 1 < diag_blocks else None
        overlap(following, (bufs[d % 2], n_full + d, d))

    lq = lam_ref[...]
    lam = (jnp.exp(jnp.sum(lq[0:1] * lq[1:2], axis=1, keepdims=True))
           - jnp.exp(jnp.sum(lq[2:3] * lq[3:4], axis=1, keepdims=True)) + lambda_init)
    a0 = a0_ref[...]
    a1 = a1_ref[...]
    o_t = (a0[:DA_VDIM] / a0[DA_VDIM:DA_VDIM + 1]
           - lam * (a1[:DA_VDIM] / a1[DA_VDIM:DA_VDIM + 1]))
    o_t = o_t * lax.rsqrt(jnp.mean(o_t * o_t, axis=0, keepdims=True) + SUBLN_EPS)
    o_t = o_t * g_ref[...] * (1.0 - lambda_init)
    o_ref[...] = o_t.T.astype(BF16)


def _diff_attention(qa, ka, va_t, lambda_qk, subln_col, layer, lambda_init, tq, tk, cq):
    b, s, width = qa.shape
    heads = width // DA_VDIM
    assert DA_VDIM == LANES and s % tq == 0 and tq % (2 * tk) == 0 and tk % cq == 0
    assert va_t.shape == (b, heads, s // tk, DA_VDIM + BF16_ROWS, tk)
    k_spec = pl.BlockSpec((None, s, DA_VDIM), lambda bi, h, i: (bi, 0, h))
    vt_spec = pl.BlockSpec((None, None) + va_t.shape[2:], lambda bi, h, i: (bi, h, 0, 0, 0))
    stat = pltpu.VMEM((1, tq), F32)
    acc = pltpu.VMEM((DA_VDIM + BF16_ROWS, tq), F32)
    scores = pltpu.VMEM((2, tk, tq), F32)
    block_max = pltpu.VMEM((2, 1, tq), F32)
    return pl.pallas_call(
        functools.partial(_diffattn_kernel, lambda_init=lambda_init, tq=tq, tk=tk, cq=cq),
        grid=(b, heads, s // tq),
        in_specs=[pl.BlockSpec((None, tq, DA_VDIM), lambda bi, h, i: (bi, i, h)),
                  k_spec, vt_spec,
                  pl.BlockSpec((None, 4, HEAD_DIM), lambda bi, h, i: (layer, 0, 0)),
                  pl.BlockSpec((None, DA_VDIM, 1), lambda bi, h, i: (layer, 0, 0))],
        out_specs=pl.BlockSpec((None, tq, DA_VDIM), lambda bi, h, i: (bi, i, h)),
        out_shape=jax.ShapeDtypeStruct((b, s, width), BF16),
        scratch_shapes=[stat, acc, stat, acc, scores, block_max, scores, block_max],
        compiler_params=_params(3),
        name="diff_attention",
    )(qa, ka, va_t, lambda_qk, subln_col)


def _swa_kernel(q_ref, kc_ref, vc_ref, kp_ref, vp_ref, sink_ref, o_ref, *, tq):
    i = pl.program_id(1)
    kcat = jnp.concatenate([kp_ref[...], kc_ref[...]], axis=0)
    vcat = jnp.concatenate([vp_ref[...], vc_ref[...]], axis=0)
    n_r = tq // WINDOW
    rows_per = SW_GROUP * WINDOW
    stacked = (n_r * rows_per, 2 * WINDOW)
    row = lax.broadcasted_iota(jnp.int32, stacked, 0)
    qi = row % WINDOW
    kj = lax.broadcasted_iota(jnp.int32, stacked, 1)
    before_start = (row < rows_per) & (i == 0) & (kj < WINDOW)
    keep = (kj > qi) & (kj <= qi + WINDOW) & jnp.logical_not(before_start)
    bias = jnp.where(keep, 0.0, -jnp.inf).astype(F32)
    bias_lo, bias_hi = bias[:, :LANES], bias[:, LANES:]
    sink_lane = lax.broadcasted_iota(jnp.int32, (n_r * rows_per, LANES), 1) == 0
    key0 = lax.broadcasted_iota(jnp.int32, (2 * WINDOW, HEAD_DIM), 0) == 0
    zero_kv = jnp.zeros((2 * WINDOW, HEAD_DIM), BF16)
    sinks = sink_ref[...] * LOG2_E
    ones = jnp.ones((2 * WINDOW, LANES), BF16)
    pad = jnp.zeros((2 * WINDOW, LANES - HEAD_DIM), BF16)
    group_w = SW_GROUP * HEAD_DIM
    for kvh in range(SW_KV_HEADS):
        cols = slice(kvh * HEAD_DIM, (kvh + 1) * HEAD_DIM)
        sink_blk = jnp.concatenate(
            [jnp.broadcast_to(sinks[:, kvh * SW_GROUP + g:kvh * SW_GROUP + g + 1], (WINDOW, LANES))
             for g in range(SW_GROUP)], axis=0)
        sink_col = jnp.concatenate([sink_blk] * n_r, axis=0)
        bias_k = jnp.concatenate([jnp.where(sink_lane, sink_col, bias_lo), bias_hi], axis=1)
        scores = []
        for r in range(n_r):
            k = jnp.where(key0, zero_kv, kcat[r * WINDOW:(r + 2) * WINDOW, cols])
            qblk = q_ref[r * WINDOW:(r + 1) * WINDOW, kvh * group_w:(kvh + 1) * group_w]
            q = jnp.concatenate([qblk[:, g * HEAD_DIM:(g + 1) * HEAD_DIM]
                                 for g in range(SW_GROUP)], axis=0)
            scores.append(lax.dot_general(q, k, NT_DIMS, preferred_element_type=F32))
        s = jnp.concatenate(scores, axis=0) + bias_k
        m = jnp.broadcast_to(jnp.max(s, axis=1, keepdims=True), (n_r * rows_per, LANES))
        p = jnp.exp2(s - jnp.concatenate([m, m], axis=1)).astype(BF16)
        pvs = []
        for r in range(n_r):
            v = jnp.where(key0, zero_kv, vcat[r * WINDOW:(r + 2) * WINDOW, cols])
            v_ones = jnp.concatenate([v, pad, ones], axis=1)
            pvs.append(jnp.dot(p[r * rows_per:(r + 1) * rows_per], v_ones,
                               preferred_element_type=F32))
        pv = jnp.concatenate(pvs, axis=0)
        o = (pv[:, :HEAD_DIM] / pv[:, LANES:LANES + HEAD_DIM]).astype(BF16)
        for r in range(n_r):
            o_ref[r * WINDOW:(r + 1) * WINDOW, kvh * group_w:(kvh + 1) * group_w] = (
                jnp.concatenate([o[(r * SW_GROUP + g) * WINDOW:(r * SW_GROUP + g + 1) * WINDOW]
                                 for g in range(SW_GROUP)], axis=1))


def _swa_attention(qs, ksw, vsw, sinks, layer, tq):
    b, s, width = qs.shape
    kv_width = ksw.shape[-1]
    per = tq // WINDOW
    row = lambda bi, i: (bi, i, 0)
    prev = lambda bi, i: (bi, jnp.maximum(i * per - 1, 0), 0)
    return pl.pallas_call(
        functools.partial(_swa_kernel, tq=tq),
        grid=(b, s // tq),
        in_specs=[pl.BlockSpec((None, tq, width), row),
                  pl.BlockSpec((None, tq, kv_width), row),
                  pl.BlockSpec((None, tq, kv_width), row),
                  pl.BlockSpec((None, WINDOW, kv_width), prev),
                  pl.BlockSpec((None, WINDOW, kv_width), prev),
                  pl.BlockSpec((None, 1, SW_Q_HEADS), lambda bi, i: (layer, 0, 0))],
        out_specs=pl.BlockSpec((None, tq, width), row),
        out_shape=jax.ShapeDtypeStruct((b, s, width), BF16),
        compiler_params=_params(2),
        name="swa_attention",
    )(qs, ksw, vsw, ksw, vsw, sinks)


def _merge_kernel(x_ref, oa_ref, ob_ref, gate_ref, mod_ref, wa_ref, wb_ref, wo_ref, o_ref):
    d = x_ref.shape[-1]
    a = jnp.dot(oa_ref[...], wa_ref[...], preferred_element_type=F32)
    bb = jnp.dot(ob_ref[...], wb_ref[...], preferred_element_type=F32)
    mixed = gate_ref[:, :d].astype(F32) * a + gate_ref[:, d:].astype(F32) * bb
    y = jnp.dot(mixed.astype(BF16), wo_ref[...], preferred_element_type=F32)
    o_ref[...] = x_ref[...] + mod_ref[2:3, :] * y


def _merge_out(x, oa, ob, gates, mod_l, wa_bf, wb_bf, wo_bf, layer, tm):
    b, s, d = x.shape
    row = lambda bi, i: (bi, i, 0)
    wspec = lambda w: _resident((None,) + w.shape[1:], lambda bi, i: (layer, 0, 0))
    return pl.pallas_call(
        _merge_kernel,
        grid=(b, s // tm),
        in_specs=[pl.BlockSpec((None, tm, d), row),
                  pl.BlockSpec((None, tm, oa.shape[-1]), row),
                  pl.BlockSpec((None, tm, ob.shape[-1]), row),
                  pl.BlockSpec((None, tm, 2 * d), row),
                  pl.BlockSpec((None, N_MOD, d), lambda bi, i: (bi, 0, 0)),
                  wspec(wa_bf), wspec(wb_bf), wspec(wo_bf)],
        out_specs=pl.BlockSpec((None, tm, d), row),
        out_shape=jax.ShapeDtypeStruct((b, s, d), F32),
        compiler_params=_params(2),
        name="merge_out",
    )(x, oa, ob, gates, mod_l, wa_bf, wb_bf, wo_bf)


def _ffn_kernel(x_ref, halo_ref, mod_ref, g_ref, wg_ref, wu_ref, cw_ref, cb_ref, wd_ref, fin_ref,
                o_ref, h_ref, act_ref, *, chunk, last_layer):
    i = pl.program_id(1)
    mod = mod_ref[...]
    shift, scale, gate = mod[3:4], mod[4:5], mod[5:6]
    gain = g_ref[...]

    def normed(rows):
        return _rms(rows, NORM_EPS) * gain * (1.0 + scale) + shift

    x = x_ref[...]
    halo = jnp.where(i > 0, normed(halo_ref[...]), 0.0)
    h_ref[0:BF16_ROWS, :] = halo.astype(BF16)
    h_ref[BF16_ROWS:, :] = normed(x).astype(BF16)

    d_ff = wg_ref.shape[-1]
    for c0 in range(0, d_ff, chunk):
        cols = slice(c0, c0 + chunk)
        g = jnp.dot(h_ref[...], wg_ref[:, cols], preferred_element_type=F32)
        u = jnp.dot(h_ref[BF16_ROWS:, :], wu_ref[:, cols], preferred_element_type=F32)
        gc = cb_ref[:, cols] + cw_ref[CONV_W - 1:CONV_W, cols] * g[BF16_ROWS:]
        for t in range(CONV_W - 1):
            back = CONV_W - 1 - t
            gc = gc + cw_ref[t:t + 1, cols] * pltpu.roll(g, back, axis=0)[BF16_ROWS:]
        act_ref[:, cols] = ((gc * jax.nn.sigmoid(gc)) * u).astype(BF16)

    y = jnp.dot(act_ref[...], wd_ref[...], preferred_element_type=F32)
    out = x + gate * y
    if last_layer:
        out = _rms(out, NORM_EPS) * fin_ref[...]
    o_ref[...] = out


def _conv_ffn(x, mod_l, g, wg_bf, wu_bf, conv_w, conv_b, wd_bf, final_g, layer, tm, last_layer):
    b, s, d = x.shape
    d_ff = wg_bf.shape[-1]
    per = tm // BF16_ROWS
    row = lambda bi, i: (bi, i, 0)
    wspec = lambda w: _resident((None,) + w.shape[1:], lambda bi, i: (layer, 0, 0))
    return pl.pallas_call(
        functools.partial(_ffn_kernel, chunk=_ffn_chunk_width(d_ff), last_layer=last_layer),
        grid=(b, s // tm),
        in_specs=[pl.BlockSpec((None, tm, d), row),
                  pl.BlockSpec((None, BF16_ROWS, d),
                               lambda bi, i: (bi, jnp.maximum(i * per - 1, 0), 0)),
                  pl.BlockSpec((None, N_MOD, d), lambda bi, i: (bi, 0, 0)),
                  pl.BlockSpec((None, 1, d), lambda bi, i: (layer, 0, 0)),
                  wspec(wg_bf), wspec(wu_bf), wspec(conv_w), wspec(conv_b), wspec(wd_bf),
                  pl.BlockSpec((1, d), lambda bi, i: (0, 0))],
        out_specs=pl.BlockSpec((None, tm, d), row),
        out_shape=jax.ShapeDtypeStruct((b, s, d), F32),
        scratch_shapes=[pltpu.VMEM((BF16_ROWS + tm, d), BF16), pltpu.VMEM((tm, d_ff), BF16)],
        compiler_params=_params(2),
        name="conv_ffn",
    )(x, x, mod_l, g, wg_bf, wu_bf, conv_w, conv_b, wd_bf, final_g)


def _ffn_chunk_width(d_ff):
    for cw in (512, 384, 256, 128):
        if d_ff % cw == 0:
            return cw
    raise ValueError(f"d_ff={d_ff} is not a multiple of {LANES}")


def kernel(x, c, positions, w_ada, b_ada, attn_norm, w_in, lambda_qk, subln_norm, sinks,
           w_branch_a, w_branch_b, w_out, ffn_norm, w_gate, w_up, conv_w, conv_b, w_down,
           final_norm):
    b, s, d = x.shape
    depth = w_in.shape[0]
    d_ff = w_gate.shape[-1]
    tm = min(s, 512)
    tm_ffn = min(s, 1024)
    tq_da, tk_da, cq_da = min(s, 2048), 512, 256
    tq_sw = min(s, 1024)

    mod = _adaln_mod(c, w_ada, b_ada)
    cos, sin_signed = _rope_tables(positions)

    w_in_bf = w_in.astype(BF16)
    wa_bf, wb_bf, wo_bf = (w.astype(BF16) for w in (w_branch_a, w_branch_b, w_out))
    wg_bf, wu_bf, wd_bf = (w.astype(BF16) for w in (w_gate, w_up, w_down))
    conv_b3 = conv_b.reshape(depth, 1, d_ff)
    attn_g = attn_norm.reshape(depth, 1, d)
    ffn_g = ffn_norm.reshape(depth, 1, d)
    subln_col = subln_norm.reshape(depth, DA_VDIM, 1)
    sinks3 = sinks.reshape(depth, 1, SW_Q_HEADS)
    final_g = final_norm.reshape(1, d)

    for l in range(depth):
        lambda_init = 0.8 - 0.6 * math.exp(-0.3 * l)
        mod_l = mod[l]
        qa, ka, va_t, qs, ksw, vsw, gates = _in_projection(x, mod_l, attn_g, cos, sin_signed,
                                                          w_in_bf, l, tk_da)
        oa = _diff_attention(qa, ka, va_t, lambda_qk, subln_col, l, lambda_init, tq_da, tk_da,
                             cq_da)
        ob = _swa_attention(qs, ksw, vsw, sinks3, l, tq_sw)
        x = _merge_out(x, oa, ob, gates, mod_l, wa_bf, wb_bf, wo_bf, l, tm_ffn)
        x = _conv_ffn(x, mod_l, ffn_g, wg_bf, wu_bf, conv_w, conv_b3, wd_bf, final_g, l, tm_ffn,
                      last_layer=(l == depth - 1))
    return x
```
